```python
import jax, jax.numpy as jnp
from jax import lax
import numpy as np

D_MODEL = 2048
BATCH = 1
SEQ = 16384
DEPTH = 4

CHUNK = 64
D_MIX = D_MODEL
D_LRU = D_MIX // 2
LRU_HEADS = 16
LRU_HEAD_DIM = D_LRU // LRU_HEADS
CONV_WIDTH = 4
LRU_C = 8.0
D_POOL = D_MIX - D_LRU
POOL_WINDOWS = (2, 4, 8, 16)
POOL_GROUPS = len(POOL_WINDOWS)
POOL_GROUP_DIM = D_POOL // POOL_GROUPS
PEER_HEADS = 8
PEER_NKEYS = 128
PEER_EXPERTS = PEER_NKEYS * PEER_NKEYS
PEER_TOPK = 16
PEER_QDIM = D_MODEL // PEER_HEADS
PEER_HALF = PEER_QDIM // 2
PEER_TOKEN_BLOCK = 128
PLE_DIM = 256
RMS_EPS = 1e-6

kernel_name = "hawk_pool_peer_hybrid"


def rmsnorm(x, g):
    xf = x.astype(jnp.float32)
    y = xf * lax.rsqrt(jnp.mean(xf * xf, axis=-1, keepdims=True) + RMS_EPS)
    return (y * g.astype(jnp.float32)).astype(x.dtype)


def causal_depthwise_conv(x, w, b):
    S = x.shape[1]
    xp = jnp.pad(x, ((0, 0), (CONV_WIDTH - 1, 0), (0, 0)))
    out = b
    for k in range(CONV_WIDTH):
        out = out + xp[:, k:k + S] * w[k]
    return out


def rg_lru(x, wa, ba, wx, bx, lam):
    B, S, _ = x.shape
    f32 = jnp.float32
    xh = x.reshape(B, S, LRU_HEADS, LRU_HEAD_DIM)
    r = jax.nn.sigmoid((jnp.einsum('bshi,hij->bshj', xh, wa).reshape(B, S, D_LRU) + ba).astype(f32))
    ig = jax.nn.sigmoid((jnp.einsum('bshi,hij->bshj', xh, wx).reshape(B, S, D_LRU) + bx).astype(f32))
    log_a = -LRU_C * r * jax.nn.softplus(-lam.astype(f32))
    a = jnp.exp(log_a)
    mult = jnp.sqrt(-jnp.expm1(2.0 * log_a))
    b_in = mult * ig * x.astype(f32)

    def combine(left, right):
        a1, b1 = left
        a2, b2 = right
        return a1 * a2, a2 * b1 + b2

    _, h = lax.associative_scan(combine, (a, b_in), axis=1)
    return h.astype(x.dtype)


def multiscale_pool(x, w, scale):
    B, S, _ = x.shape
    f32 = jnp.float32
    xg = x.reshape(B, S, POOL_GROUPS, POOL_GROUP_DIM).astype(f32)
    cs = jnp.pad(jnp.cumsum(xg, axis=1), ((0, 0), (1, 0), (0, 0), (0, 0)))
    win = jnp.array(POOL_WINDOWS, dtype=jnp.int32)
    t = jnp.arange(S, dtype=jnp.int32)[:, None]
    start = jnp.maximum(t + 1 - win[None, :], 0)
    count = (t + 1 - start).astype(f32)
    lo = cs[:, start, jnp.arange(POOL_GROUPS)[None, :]]
    mean = (cs[:, 1:] - lo) / count[None, :, :, None]
    d = (mean - xg).astype(x.dtype)
    y = jnp.einsum('bsgc,gcd->bsgd', d, w).reshape(B, S, D_POOL)
    return y * scale


def peer(x, wq, keys, u, v):
    B, S, D = x.shape
    xt = x.reshape(-1, PEER_TOKEN_BLOCK, D)

    def block(xb):
        T = xb.shape[0]
        q = (xb @ wq).reshape(T, PEER_HEADS, 2, PEER_HALF)
        s = jnp.einsum('thcd,cnd->thcn', q, keys).astype(jnp.float32)
        sv, si = lax.top_k(s, PEER_TOPK)
        cand = sv[:, :, 0, :, None] + sv[:, :, 1, None, :]
        cand_idx = si[:, :, 0, :, None] * PEER_NKEYS + si[:, :, 1, None, :]
        cv, ci = lax.top_k(cand.reshape(T, PEER_HEADS, PEER_TOPK * PEER_TOPK), PEER_TOPK)
        eidx = jnp.take_along_axis(cand_idx.reshape(T, PEER_HEADS, PEER_TOPK * PEER_TOPK), ci, axis=-1)
        g = jax.nn.softmax(cv, axis=-1)
        ue = u[eidx]
        act = jax.nn.gelu(jnp.einsum('thkd,td->thk', ue, xb).astype(jnp.float32))
        ve = v[eidx]
        return jnp.einsum('thk,thkd->td', (g * act).astype(xb.dtype), ve)

    y = lax.map(block, xt)
    return y.reshape(B, S, D)


def setup_inputs(seed: int = 0) -> dict:
    key = jax.random.key(seed)
    ks = jax.random.split(key, 24)
    f32 = jnp.float32

    def nrm(k, shape, scale):
        return jax.random.normal(k, shape, dtype=f32) * scale

    def gain(k, shape):
        return 1.0 + 0.05 * jax.random.normal(k, shape, dtype=f32)

    a_c = jax.random.uniform(ks[8], (DEPTH, D_LRU), dtype=f32, minval=0.9, maxval=0.999)
    s = a_c ** (1.0 / LRU_C)
    lam = jnp.log(s) - jnp.log1p(-s)

    return {
        "x": nrm(ks[0], (BATCH, SEQ, D_MODEL), 1.0),
        "p": nrm(ks[1], (DEPTH, BATCH, SEQ, PLE_DIM), 1.0),
        "norm_mix_g": gain(ks[2], (DEPTH, D_MODEL)),
        "w_in": nrm(ks[3], (DEPTH, D_MODEL, 2 * D_LRU + D_POOL), D_MODEL ** -0.5),
        "conv_w": nrm(ks[4], (DEPTH, CONV_WIDTH, D_LRU), CONV_WIDTH ** -0.5),
        "conv_b": nrm(ks[5], (DEPTH, D_LRU), 0.02),
        "lru_wa": nrm(ks[6], (DEPTH, LRU_HEADS, LRU_HEAD_DIM, LRU_HEAD_DIM), LRU_HEAD_DIM ** -0.5),
        "lru_ba": nrm(ks[7], (DEPTH, D_LRU), 0.1),
        "lru_wx": nrm(ks[9], (DEPTH, LRU_HEADS, LRU_HEAD_DIM, LRU_HEAD_DIM), LRU_HEAD_DIM ** -0.5),
        "lru_bx": nrm(ks[10], (DEPTH, D_LRU), 0.1),
        "lru_lam": lam,
        "pool_w": nrm(ks[11], (DEPTH, POOL_GROUPS, POOL_GROUP_DIM, POOL_GROUP_DIM), POOL_GROUP_DIM ** -0.5),
        "pool_scale": jax.random.uniform(ks[12], (DEPTH, D_POOL), dtype=f32, minval=0.5, maxval=1.5),
        "w_out": nrm(ks[13], (DEPTH, D_MIX, D_MODEL), D_MIX ** -0.5),
        "norm_ffn_g": gain(ks[14], (DEPTH, D_MODEL)),
        "peer_wq": nrm(ks[15], (DEPTH, D_MODEL, PEER_HEADS * PEER_QDIM), D_MODEL ** -0.5),
        "peer_keys": nrm(ks[16], (DEPTH, 2, PEER_NKEYS, PEER_HALF), PEER_HALF ** -0.5),
        "peer_u": nrm(ks[17], (DEPTH, PEER_EXPERTS, D_MODEL), D_MODEL ** -0.5),
        "peer_v": nrm(ks[18], (DEPTH, PEER_EXPERTS, D_MODEL), PEER_HEADS ** -0.5),
        "norm_ple_g": gain(ks[19], (DEPTH, D_MODEL)),
        "ple_wp": nrm(ks[20], (DEPTH, PLE_DIM, D_MODEL), PLE_DIM ** -0.5),
        "ple_wg": nrm(ks[21], (DEPTH, D_MODEL, D_MODEL), D_MODEL ** -0.5),
        "final_g": gain(ks[22], (D_MODEL,)),
    }


def reference(x, p, norm_mix_g, w_in, conv_w, conv_b, lru_wa, lru_ba, lru_wx, lru_bx, lru_lam,
              pool_w, pool_scale, w_out, norm_ffn_g, peer_wq, peer_keys, peer_u, peer_v,
              norm_ple_g, ple_wp, ple_wg, final_g):
    h = x
    for i in range(DEPTH):
        hn = rmsnorm(h, norm_mix_g[i])
        z = hn @ w_in[i]
        x_lru = z[..., :D_LRU]
        y_gate = z[..., D_LRU:2 * D_LRU]
        x_pool = z[..., 2 * D_LRU:]
        x_lru = causal_depthwise_conv(x_lru, conv_w[i], conv_b[i])
        lru_out = rg_lru(x_lru, lru_wa[i], lru_ba[i], lru_wx[i], lru_bx[i], lru_lam[i]) * jax.nn.gelu(y_gate)
        pool_out = multiscale_pool(x_pool, pool_w[i], pool_scale[i])
        h = h + jnp.concatenate([lru_out, pool_out], axis=-1) @ w_out[i]
        h = h + peer(rmsnorm(h, norm_ffn_g[i]), peer_wq[i], peer_keys[i], peer_u[i], peer_v[i])
        gate = jax.nn.sigmoid(rmsnorm(h, norm_ple_g[i]) @ ple_wg[i])
        h = h + gate * (p[i] @ ple_wp[i])
    return rmsnorm(h, final_g)
```

```python
import functools
import math

import jax
import jax.numpy as jnp
from jax import lax
from jax.experimental import pallas as pl
from jax.experimental.pallas import tpu as pltpu

F32 = jnp.float32
BF16 = jnp.bfloat16

CONV_WIDTH = 4
LRU_HEADS = 16
LRU_C = 8.0
POOL_WINDOWS = (2, 4, 8, 16)
PEER_HEADS = 8
PEER_NKEYS = 128
PEER_TOPK = 16
RMS_EPS = 1e-6

SUBLANES = 8
LANES = 128
MXU_DIM = 256
VMEM_LIMIT_BYTES = 56 * 1024 * 1024

MIX_TOKENS = 256
ROUTE_TOKENS = 256
PEER_TOKENS = 512
PEER_EXPERTS_PER_STEP = 512
PLE_TOKENS = 512

CONV_HALO = SUBLANES
POOL_HALO = 2 * SUBLANES
NEG_INF = float("-inf")


def _rmsnorm(x, g):
    return x * lax.rsqrt(jnp.mean(x * x, axis=-1, keepdims=True) + RMS_EPS) * g


def _dot(a, b):
    return jnp.dot(a, b, preferred_element_type=F32)


def _dot_nt(a, b):
    return lax.dot_general(a, b, (((1,), (1,)), ((), ())), preferred_element_type=F32)


def _const_spec(shape):
    zeros = (0,) * len(shape)
    return pl.BlockSpec(shape, lambda *_: zeros, pipeline_mode=pl.Buffered(1))


def _params(semantics):
    return pltpu.CompilerParams(dimension_semantics=semantics, vmem_limit_bytes=VMEM_LIMIT_BYTES)


def _mix_kernel(h_ref, g_ref, win_ref, convw_ref, convb_ref, wa_ref, ba_ref, wx_ref, bx_ref, lam_ref,
                poolw_ref, pscale_ref, wout_ref, o_ref,
                xbuf, pbuf, hstate, cat_ref, *, d_lru, d_pool):
    tm = h_ref.shape[0]
    step = pl.program_id(0)

    @pl.when(step == 0)
    def _():
        xbuf[0:CONV_HALO, :] = jnp.zeros((CONV_HALO, d_lru), F32)
        pbuf[0:POOL_HALO, :] = jnp.zeros((POOL_HALO, d_pool), F32)
        hstate[...] = jnp.zeros_like(hstate)

    h = h_ref[...]
    hn = _rmsnorm(h, g_ref[...]).astype(BF16)
    z = _dot(hn, win_ref[...])
    xbuf[CONV_HALO:CONV_HALO + tm, :] = z[:, :d_lru]
    y_gate = z[:, d_lru:2 * d_lru]
    pbuf[POOL_HALO:POOL_HALO + tm, :] = z[:, 2 * d_lru:]

    x = convb_ref[...] + xbuf[CONV_HALO:CONV_HALO + tm, :] * convw_ref[CONV_WIDTH - 1:CONV_WIDTH, :]
    for k in range(1, CONV_WIDTH):
        x = x + xbuf[CONV_HALO - k:CONV_HALO - k + tm, :] * convw_ref[CONV_WIDTH - 1 - k:CONV_WIDTH - k, :]
    xbuf[0:CONV_HALO, :] = xbuf[tm:tm + CONV_HALO, :]

    xb = x.astype(BF16)
    n_blk = d_lru // MXU_DIM
    r_parts, i_parts = [], []
    for b in range(n_blk):
        xs = xb[:, b * MXU_DIM:(b + 1) * MXU_DIM]
        r_parts.append(_dot(xs, wa_ref[b]))
        i_parts.append(_dot(xs, wx_ref[b]))
    r = jax.nn.sigmoid(jnp.concatenate(r_parts, axis=1) + ba_ref[...])
    ig = jax.nn.sigmoid(jnp.concatenate(i_parts, axis=1) + bx_ref[...])
    lam = lam_ref[...]
    softplus_neg_lam = jnp.maximum(-lam, 0.0) + jnp.log(1.0 + jnp.exp(-jnp.abs(lam)))
    log_a = (-LRU_C) * r * softplus_neg_lam
    a = jnp.exp(log_a)
    mult = jnp.sqrt(1.0 - jnp.exp(2.0 * log_a))
    b_in = mult * ig * x

    row = lax.broadcasted_iota(jnp.int32, (tm, d_lru), 0)
    shift = 1
    while shift < tm:
        if shift < SUBLANES:
            keep = row >= shift
            a_prev = jnp.where(keep, pltpu.roll(a, shift, axis=0), 1.0)
            b_prev = jnp.where(keep, pltpu.roll(b_in, shift, axis=0), 0.0)
        else:
            a_prev = jnp.concatenate([jnp.ones((shift, d_lru), F32), a[:tm - shift]], axis=0)
            b_prev = jnp.concatenate([jnp.zeros((shift, d_lru), F32), b_in[:tm - shift]], axis=0)
        b_in = a * b_prev + b_in
        a = a * a_prev
        shift *= 2
    hs = a * hstate[...] + b_in
    hstate[...] = hs[tm - 1:tm, :]
    cat_ref[:, 0:d_lru] = (hs * jax.nn.gelu(y_gate)).astype(BF16)

    n_groups = len(POOL_WINDOWS)
    gdim = d_pool // n_groups
    frames = step * tm + 1 + lax.broadcasted_iota(jnp.int32, (tm, gdim), 0)
    for gi, win in enumerate(POOL_WINDOWS):
        cols = slice(gi * gdim, (gi + 1) * gdim)
        cur = pbuf[POOL_HALO:POOL_HALO + tm, cols]
        s = cur
        for k in range(1, win):
            s = s + pbuf[POOL_HALO - k:POOL_HALO - k + tm, cols]
        count = jnp.minimum(frames, win).astype(F32)
        d = (s / count - cur).astype(BF16)
        y = _dot(d, poolw_ref[gi]) * pscale_ref[:, cols]
        cat_ref[:, d_lru + gi * gdim:d_lru + (gi + 1) * gdim] = y.astype(BF16)
    pbuf[0:POOL_HALO, :] = pbuf[tm:tm + POOL_HALO, :]

    o_ref[...] = h + _dot(cat_ref[...], wout_ref[...])


def _mix(h, g, w_in, conv_w, conv_b, wa_bd, ba, wx_bd, bx, lam, pool_w, pool_scale, w_out):
    t, d = h.shape
    d_lru = conv_w.shape[1]
    d_pool = pool_scale.shape[1]
    tm = min(MIX_TOKENS, t)
    row_spec = pl.BlockSpec((tm, d), lambda i: (i, 0))
    args = (g, w_in, conv_w, conv_b, wa_bd, ba, wx_bd, bx, lam, pool_w, pool_scale, w_out)
    return pl.pallas_call(
        functools.partial(_mix_kernel, d_lru=d_lru, d_pool=d_pool),
        out_shape=jax.ShapeDtypeStruct((t, d), F32),
        grid=(t // tm,),
        in_specs=[row_spec] + [_const_spec(a.shape) for a in args],
        out_specs=row_spec,
        scratch_shapes=[
            pltpu.VMEM((tm + CONV_HALO, d_lru), F32),
            pltpu.VMEM((tm + POOL_HALO, d_pool), F32),
            pltpu.VMEM((1, d_lru), F32),
            pltpu.VMEM((tm, d_lru + d_pool), BF16),
        ],
        compiler_params=_params(("arbitrary",)),
        name="mix",
    )(h, *args)


def _sort16_network():
    pairs = []

    def merge(lo, n, r):
        step = r * 2
        if step < n:
            merge(lo, n, step)
            merge(lo + r, n, step)
            for i in range(lo + r, lo + n - r, step):
                pairs.append((i, i + r))
        else:
            pairs.append((lo, lo + r))

    def sort(lo, n):
        if n > 1:
            m = n // 2
            sort(lo, m)
            sort(lo + m, m)
            merge(lo, n, 1)

    sort(0, PEER_TOPK)
    return tuple(pairs)


_SORT16 = _sort16_network()


def _sort_desc(vals):
    vals = list(vals)
    for i, j in _SORT16:
        hi = jnp.maximum(vals[i], vals[j])
        lo = jnp.minimum(vals[i], vals[j])
        vals[i], vals[j] = hi, lo
    return vals


def _bitonic_desc(vals):
    vals = list(vals)
    n = len(vals)
    gap = n // 2
    while gap >= 1:
        for i in range(n):
            if (i // gap) % 2 == 0:
                hi = jnp.maximum(vals[i], vals[i + gap])
                lo = jnp.minimum(vals[i], vals[i + gap])
                vals[i], vals[i + gap] = hi, lo
        gap //= 2
    return vals


def _top16_rows(s_ref, idx):
    groups = [s_ref[idx, pl.ds(SUBLANES * v, SUBLANES), :] for v in range(PEER_NKEYS // SUBLANES)]
    vals = _sort_desc(groups)
    shift = SUBLANES // 2
    while shift >= 1:
        other = [pltpu.roll(v, shift, axis=0) for v in vals]
        merged = [jnp.maximum(vals[k], other[PEER_TOPK - 1 - k]) for k in range(PEER_TOPK)]
        vals = _bitonic_desc(merged)
        shift //= 2
    return vals


def _pack_rows(vals):
    sub = lax.broadcasted_iota(jnp.int32, vals[0].shape, 0)
    out = vals[SUBLANES - 1]
    for k in range(SUBLANES - 2, -1, -1):
        out = jnp.where(sub == k, vals[k], out)
    return out


def _route_kernel(h_ref, g_ref, wq_ref, keys_ref, xn_ref, s1_ref, s2_ref, stat_ref, s_scr):
    tr = h_ref.shape[0]
    xn = _rmsnorm(h_ref[...], g_ref[...]).astype(BF16)
    xn_ref[...] = xn
    q = _dot(xn, wq_ref[...]).astype(BF16)
    half = keys_ref.shape[2]
    for hd in range(PEER_HEADS):
        for c in range(2):
            col = (hd * 2 + c) * half
            s = _dot_nt(keys_ref[c], q[:, col:col + half])
            s_scr[hd * 2 + c] = s
            if c == 0:
                s1_ref[hd] = s
            else:
                s2_ref[hd] = s

    def per_head(hd, carry):
        sv1 = _top16_rows(s_scr, 2 * hd)
        sv2 = _top16_rows(s_scr, 2 * hd + 1)
        p1_hi = _pack_rows(sv1[SUBLANES:])
        p2_lo = _pack_rows(sv2[:SUBLANES])
        p2_hi = _pack_rows(sv2[SUBLANES:])
        cands = [sv1[0] + p2_lo, sv1[0] + p2_hi, p1_hi + sv2[0]]
        cands += [sv1[a] + p2_lo for a in range(1, SUBLANES)]
        cand = jnp.concatenate(cands, axis=0)
        m_top = sv1[0][0:1, :] + sv2[0][0:1, :]
        rem = cand
        tau = jnp.full((1, tr), NEG_INF, F32)
        cum = jnp.zeros((1, tr), F32)
        for _ in range(PEER_TOPK):
            m = jnp.max(rem, axis=0, keepdims=True)
            eq = rem == m
            tau = jnp.where(cum < PEER_TOPK, m, tau)
            cum = cum + jnp.sum(eq.astype(F32), axis=0, keepdims=True)
            rem = jnp.where(eq, NEG_INF, rem)
        zsum = jnp.sum(jnp.where(cand >= tau, jnp.exp(cand - m_top), 0.0), axis=0, keepdims=True)
        inv_z = 1.0 / zsum
        rows = [tau, sv1[0][0:1, :], sv2[0][0:1, :], inv_z]
        rows += [jnp.zeros((1, tr), F32)] * (SUBLANES - len(rows))
        stat_ref[hd] = jnp.concatenate(rows, axis=0)
        return carry

    lax.fori_loop(0, PEER_HEADS, per_head, 0)


def _route(h, g, wq, keys):
    t, d = h.shape
    tr = min(ROUTE_TOKENS, t)
    nk = keys.shape[1]
    tab_spec = pl.BlockSpec((PEER_HEADS, nk, tr), lambda i: (0, 0, i))
    return pl.pallas_call(
        _route_kernel,
        out_shape=(
            jax.ShapeDtypeStruct((t, d), BF16),
            jax.ShapeDtypeStruct((PEER_HEADS, nk, t), F32),
            jax.ShapeDtypeStruct((PEER_HEADS, nk, t), F32),
            jax.ShapeDtypeStruct((PEER_HEADS, SUBLANES, t), F32),
        ),
        grid=(t // tr,),
        in_specs=[pl.BlockSpec((tr, d), lambda i: (i, 0)), _const_spec(g.shape), _const_spec(wq.shape),
                  _const_spec(keys.shape)],
        out_specs=(
            pl.BlockSpec((tr, d), lambda i: (i, 0)),
            tab_spec,
            tab_spec,
            pl.BlockSpec((PEER_HEADS, SUBLANES, tr), lambda i: (0, 0, i)),
        ),
        scratch_shapes=[pltpu.VMEM((2 * PEER_HEADS, nk, tr), F32)],
        compiler_params=_params(("arbitrary",)),
        name="route",
    )(h, g, wq, keys)


def _peer_kernel(h_ref, xn_ref, u_ref, vt_ref, s1_ref, s2_ref, stat_ref, o_ref, c_scr, d_scr, e2_scr, acc_ref,
                 *, nkeys):
    j = pl.program_id(1)
    te = u_ref.shape[0]

    @pl.when(j == 0)
    def _():
        acc_ref[...] = jnp.zeros_like(acc_ref)
        for hd in range(PEER_HEADS):
            tau = stat_ref[hd, 0:1, :]
            m1 = stat_ref[hd, 1:2, :]
            m2 = stat_ref[hd, 2:3, :]
            inv_z = stat_ref[hd, 3:4, :]
            s1 = s1_ref[hd]
            c_scr[hd] = tau - s1
            d_scr[hd] = jnp.exp(s1 - m1) * inv_z
            e2_scr[hd] = jnp.exp(s2_ref[hd] - m2)

    a_t = _dot_nt(u_ref[...], xn_ref[...])
    act = jax.nn.gelu(a_t)
    blocks = []
    for ii in range(te // nkeys):
        i = j * (te // nkeys) + ii
        w = None
        for hd in range(PEER_HEADS):
            c_row = c_scr[hd, pl.ds(i, 1), :]
            d_row = d_scr[hd, pl.ds(i, 1), :]
            contrib = jnp.where(s2_ref[hd] >= c_row, e2_scr[hd] * d_row, 0.0)
            w = contrib if w is None else w + contrib
        blocks.append((w * act[ii * nkeys:(ii + 1) * nkeys, :]).astype(BF16))
    p_t = jnp.concatenate(blocks, axis=0)
    acc_ref[...] += _dot(vt_ref[...], p_t)

    @pl.when(j == pl.num_programs(1) - 1)
    def _():
        o_ref[...] = h_ref[...] + acc_ref[...].T


def _peer(h, xn, u, vt, s1t, s2t, stats):
    t, d = h.shape
    n_exp = u.shape[0]
    nk = s1t.shape[1]
    tt = min(PEER_TOKENS, t)
    te = PEER_EXPERTS_PER_STEP
    tab_spec = pl.BlockSpec((PEER_HEADS, nk, tt), lambda i, j: (0, 0, i))
    return pl.pallas_call(
        functools.partial(_peer_kernel, nkeys=nk),
        out_shape=jax.ShapeDtypeStruct((t, d), F32),
        grid=(t // tt, n_exp // te),
        in_specs=[
            pl.BlockSpec((tt, d), lambda i, j: (i, 0)),
            pl.BlockSpec((tt, d), lambda i, j: (i, 0)),
            pl.BlockSpec((te, d), lambda i, j: (j, 0)),
            pl.BlockSpec((d, te), lambda i, j: (0, j)),
            tab_spec,
            tab_spec,
            pl.BlockSpec((PEER_HEADS, SUBLANES, tt), lambda i, j: (0, 0, i)),
        ],
        out_specs=pl.BlockSpec((tt, d), lambda i, j: (i, 0)),
        scratch_shapes=[
            pltpu.VMEM((PEER_HEADS, nk, tt), F32),
            pltpu.VMEM((PEER_HEADS, nk, tt), F32),
            pltpu.VMEM((PEER_HEADS, nk, tt), F32),
            pltpu.VMEM((d, tt), F32),
        ],
        compiler_params=_params(("arbitrary", "arbitrary")),
        name="peer",
    )(h, xn, u, vt, s1t, s2t, stats)


def _ple_kernel(h_ref, p_ref, g_ref, wg_ref, wp_ref, fg_ref, o_ref, *, final_norm):
    h = h_ref[...]
    gate = jax.nn.sigmoid(_dot(_rmsnorm(h, g_ref[...]).astype(BF16), wg_ref[...]))
    out = h + gate * _dot(p_ref[...].astype(BF16), wp_ref[...])
    if final_norm:
        out = _rmsnorm(out, fg_ref[...])
    o_ref[...] = out


def _ple(h, p, g, wg, wp, final_g, final_norm):
    t, d = h.shape
    tm = min(PLE_TOKENS, t)
    return pl.pallas_call(
        functools.partial(_ple_kernel, final_norm=final_norm),
        out_shape=jax.ShapeDtypeStruct((t, d), F32),
        grid=(t // tm,),
        in_specs=[pl.BlockSpec((tm, d), lambda i: (i, 0)), pl.BlockSpec((tm, p.shape[1]), lambda i: (i, 0)),
                  _const_spec(g.shape), _const_spec(wg.shape), _const_spec(wp.shape), _const_spec(final_g.shape)],
        out_specs=pl.BlockSpec((tm, d), lambda i: (i, 0)),
        compiler_params=_params(("arbitrary",)),
        name="ple",
    )(h, p, g, wg, wp, final_g)


def _block_diag(w, group):
    heads, n, _ = w.shape
    w = w.reshape(heads // group, group, n, n)
    eye = jnp.eye(group, dtype=w.dtype)
    return jnp.einsum("bgij,gk->bgikj", w, eye).reshape(heads // group, group * n, group * n)


def kernel(x, p, norm_mix_g, w_in, conv_w, conv_b, lru_wa, lru_ba, lru_wx, lru_bx, lru_lam, pool_w, pool_scale,
           w_out, norm_ffn_g, peer_wq, peer_keys, peer_u, peer_v, norm_ple_g, ple_wp, ple_wg, final_g):
    batch, seq, d = x.shape
    depth = w_in.shape[0]
    lru_hd = lru_wa.shape[-1]
    row = lambda v: v.reshape(1, -1)
    outs = []
    for b in range(batch):
        h = x[b]
        for i in range(depth):
            group = MXU_DIM // lru_hd
            h = _mix(h, row(norm_mix_g[i]), w_in[i].astype(BF16), conv_w[i], row(conv_b[i]),
                     _block_diag(lru_wa[i], group).astype(BF16), row(lru_ba[i]),
                     _block_diag(lru_wx[i], group).astype(BF16), row(lru_bx[i]), row(lru_lam[i]),
                     pool_w[i].astype(BF16), row(pool_scale[i]), w_out[i].astype(BF16))
            xn, s1t, s2t, stats = _route(h, row(norm_ffn_g[i]), peer_wq[i].astype(BF16), peer_keys[i].astype(BF16))
            h = _peer(h, xn, peer_u[i].astype(BF16), peer_v[i].T.astype(BF16), s1t, s2t, stats)
            h = _ple(h, p[i, b], row(norm_ple_g[i]), ple_wg[i].astype(BF16), ple_wp[i].astype(BF16),
                     row(final_g), final_norm=(i == depth - 1))
        outs.append(h)
    return jnp.stack(outs, axis=0)
```

```python
import functools
import math

import jax
import jax.numpy as jnp
from jax import lax
from jax.experimental import pallas as pl
from jax.experimental.pallas import tpu as pltpu

F32 = jnp.float32
BF16 = jnp.bfloat16

CONV_WIDTH = 4
LRU_HEADS = 16
LRU_C = 8.0
POOL_WINDOWS = (2, 4, 8, 16)
PEER_HEADS = 8
PEER_NKEYS = 128
PEER_TOPK = 16
RMS_EPS = 1e-6

SUBLANES = 8
LANES = 128
MXU_DIM = 256
VMEM_LIMIT_BYTES = 56 * 1024 * 1024

MIX_TOKENS = 256
ROUTE_TOKENS = 256
PEER_TOKENS = 512
PEER_EXPERTS_PER_STEP = 1024
PLE_TOKENS = 512

CONV_HALO = SUBLANES
POOL_HALO = 2 * SUBLANES
NEG_INF = float("-inf")


def _rmsnorm(x, g):
    return x * lax.rsqrt(jnp.mean(x * x, axis=-1, keepdims=True) + RMS_EPS) * g


def _dot(a, b):
    return jnp.dot(a, b, preferred_element_type=F32)


def _dot_nt(a, b):
    return lax.dot_general(a, b, (((1,), (1,)), ((), ())), preferred_element_type=F32)


def _const_spec(shape):
    zeros = (0,) * len(shape)
    return pl.BlockSpec(shape, lambda *_: zeros, pipeline_mode=pl.Buffered(1))


def _params(semantics):
    return pltpu.CompilerParams(dimension_semantics=semantics, vmem_limit_bytes=VMEM_LIMIT_BYTES)


def _mix_kernel(h_ref, g_ref, win_ref, convw_ref, convb_ref, wa_ref, ba_ref, wx_ref, bx_ref, lam_ref,
                poolw_ref, pscale_ref, wout_ref, o_ref,
                xbuf, pbuf, hstate, cat_ref, *, d_lru, d_pool):
    tm = h_ref.shape[0]
    step = pl.program_id(0)

    @pl.when(step == 0)
    def _():
        xbuf[0:CONV_HALO, :] = jnp.zeros((CONV_HALO, d_lru), F32)
        pbuf[0:POOL_HALO, :] = jnp.zeros((POOL_HALO, d_pool), F32)
        hstate[...] = jnp.zeros_like(hstate)

    h = h_ref[...]
    hn = _rmsnorm(h, g_ref[...]).astype(BF16)
    z = _dot(hn, win_ref[...])
    xbuf[CONV_HALO:CONV_HALO + tm, :] = z[:, :d_lru]
    y_gate = z[:, d_lru:2 * d_lru]
    pbuf[POOL_HALO:POOL_HALO + tm, :] = z[:, 2 * d_lru:]

    x = convb_ref[...] + xbuf[CONV_HALO:CONV_HALO + tm, :] * convw_ref[CONV_WIDTH - 1:CONV_WIDTH, :]
    for k in range(1, CONV_WIDTH):
        x = x + xbuf[CONV_HALO - k:CONV_HALO - k + tm, :] * convw_ref[CONV_WIDTH - 1 - k:CONV_WIDTH - k, :]
    xbuf[0:CONV_HALO, :] = xbuf[tm:tm + CONV_HALO, :]

    xb = x.astype(BF16)
    n_blk = d_lru // MXU_DIM
    r_parts, i_parts = [], []
    for b in range(n_blk):
        xs = xb[:, b * MXU_DIM:(b + 1) * MXU_DIM]
        r_parts.append(_dot(xs, wa_ref[b]))
        i_parts.append(_dot(xs, wx_ref[b]))
    r = jax.nn.sigmoid(jnp.concatenate(r_parts, axis=1) + ba_ref[...])
    ig = jax.nn.sigmoid(jnp.concatenate(i_parts, axis=1) + bx_ref[...])
    lam = lam_ref[...]
    softplus_neg_lam = jnp.maximum(-lam, 0.0) + jnp.log(1.0 + jnp.exp(-jnp.abs(lam)))
    log_a = (-LRU_C) * r * softplus_neg_lam
    a = jnp.exp(log_a)
    mult = jnp.sqrt(1.0 - jnp.exp(2.0 * log_a))
    b_in = mult * ig * x

    row = lax.broadcasted_iota(jnp.int32, (tm, d_lru), 0)
    shift = 1
    while shift < tm:
        if shift < SUBLANES:
            keep = row >= shift
            a_prev = jnp.where(keep, pltpu.roll(a, shift, axis=0), 1.0)
            b_prev = jnp.where(keep, pltpu.roll(b_in, shift, axis=0), 0.0)
        else:
            a_prev = jnp.concatenate([jnp.ones((shift, d_lru), F32), a[:tm - shift]], axis=0)
            b_prev = jnp.concatenate([jnp.zeros((shift, d_lru), F32), b_in[:tm - shift]], axis=0)
        b_in = a * b_prev + b_in
        a = a * a_prev
        shift *= 2
    hs = a * hstate[...] + b_in
    hstate[...] = hs[tm - 1:tm, :]
    cat_ref[:, 0:d_lru] = (hs * jax.nn.gelu(y_gate)).astype(BF16)

    n_groups = len(POOL_WINDOWS)
    gdim = d_pool // n_groups
    frames = step * tm + 1 + lax.broadcasted_iota(jnp.int32, (tm, gdim), 0)
    for gi, win in enumerate(POOL_WINDOWS):
        cols = slice(gi * gdim, (gi + 1) * gdim)
        cur = pbuf[POOL_HALO:POOL_HALO + tm, cols]
        s = cur
        for k in range(1, win):
            s = s + pbuf[POOL_HALO - k:POOL_HALO - k + tm, cols]
        count = jnp.minimum(frames, win).astype(F32)
        d = (s / count - cur).astype(BF16)
        y = _dot(d, poolw_ref[gi]) * pscale_ref[:, cols]
        cat_ref[:, d_lru + gi * gdim:d_lru + (gi + 1) * gdim] = y.astype(BF16)
    pbuf[0:POOL_HALO, :] = pbuf[tm:tm + POOL_HALO, :]

    o_ref[...] = h + _dot(cat_ref[...], wout_ref[...])


def _mix(h, g, w_in, conv_w, conv_b, wa_bd, ba, wx_bd, bx, lam, pool_w, pool_scale, w_out):
    t, d = h.shape
    d_lru = conv_w.shape[1]
    d_pool = pool_scale.shape[1]
    tm = min(MIX_TOKENS, t)
    row_spec = pl.BlockSpec((tm, d), lambda i: (i, 0))
    args = (g, w_in, conv_w, conv_b, wa_bd, ba, wx_bd, bx, lam, pool_w, pool_scale, w_out)
    return pl.pallas_call(
        functools.partial(_mix_kernel, d_lru=d_lru, d_pool=d_pool),
        out_shape=jax.ShapeDtypeStruct((t, d), F32),
        grid=(t // tm,),
        in_specs=[row_spec] + [_const_spec(a.shape) for a in args],
        out_specs=row_spec,
        scratch_shapes=[
            pltpu.VMEM((tm + CONV_HALO, d_lru), F32),
            pltpu.VMEM((tm + POOL_HALO, d_pool), F32),
            pltpu.VMEM((1, d_lru), F32),
            pltpu.VMEM((tm, d_lru + d_pool), BF16),
        ],
        compiler_params=_params(("arbitrary",)),
        name="mix",
    )(h, *args)


def _sort16_network():
    pairs = []

    def merge(lo, n, r):
        step = r * 2
        if step < n:
            merge(lo, n, step)
            merge(lo + r, n, step)
            for i in range(lo + r, lo + n - r, step):
                pairs.append((i, i + r))
        else:
            pairs.append((lo, lo + r))

    def sort(lo, n):
        if n > 1:
            m = n // 2
            sort(lo, m)
            sort(lo + m, m)
            merge(lo, n, 1)

    sort(0, PEER_TOPK)
    return tuple(pairs)


_SORT16 = _sort16_network()


def _sort_desc(vals):
    vals = list(vals)
    for i, j in _SORT16:
        hi = jnp.maximum(vals[i], vals[j])
        lo = jnp.minimum(vals[i], vals[j])
        vals[i], vals[j] = hi, lo
    return vals


def _bitonic_desc(vals):
    vals = list(vals)
    n = len(vals)
    gap = n // 2
    while gap >= 1:
        for i in range(n):
            if (i // gap) % 2 == 0:
                hi = jnp.maximum(vals[i], vals[i + gap])
                lo = jnp.minimum(vals[i], vals[i + gap])
                vals[i], vals[i + gap] = hi, lo
        gap //= 2
    return vals


def _top16_rows(s_ref, idx):
    groups = [s_ref[idx, pl.ds(SUBLANES * v, SUBLANES), :] for v in range(PEER_NKEYS // SUBLANES)]
    vals = _sort_desc(groups)
    shift = SUBLANES // 2
    while shift >= 1:
        other = [pltpu.roll(v, shift, axis=0) for v in vals]
        merged = [jnp.maximum(vals[k], other[PEER_TOPK - 1 - k]) for k in range(PEER_TOPK)]
        vals = _bitonic_desc(merged)
        shift //= 2
    return vals


def _pack_rows(vals):
    sub = lax.broadcasted_iota(jnp.int32, vals[0].shape, 0)
    out = vals[SUBLANES - 1]
    for k in range(SUBLANES - 2, -1, -1):
        out = jnp.where(sub == k, vals[k], out)
    return out


def _route_kernel(h_ref, g_ref, wq_ref, keys_ref, xn_ref, r_ref, e2_ref, n_ref, d_ref, s_scr):
    tr = h_ref.shape[0]
    xn = _rmsnorm(h_ref[...], g_ref[...]).astype(BF16)
    xn_ref[...] = xn
    q = _dot(xn, wq_ref[...]).astype(BF16)
    half = keys_ref.shape[2]
    for hd in range(PEER_HEADS):
        for c in range(2):
            col = (hd * 2 + c) * half
            s_scr[hd * 2 + c] = _dot_nt(keys_ref[c], q[:, col:col + half])

    n_groups = PEER_NKEYS // SUBLANES

    def per_head(hd, carry):
        sv1 = _top16_rows(s_scr, 2 * hd)
        sv2 = _top16_rows(s_scr, 2 * hd + 1)
        p1_hi = _pack_rows(sv1[SUBLANES:])
        p2_lo = _pack_rows(sv2[:SUBLANES])
        p2_hi = _pack_rows(sv2[SUBLANES:])
        cands = [sv1[0] + p2_lo, sv1[0] + p2_hi, p1_hi + sv2[0]]
        cands += [sv1[a] + p2_lo for a in range(1, SUBLANES)]
        cand = jnp.concatenate(cands, axis=0)
        m1, m2 = sv1[0], sv2[0]
        m_top = m1[0:1, :] + m2[0:1, :]
        rem = cand
        tau = jnp.full((1, tr), NEG_INF, F32)
        cum = jnp.zeros((1, tr), F32)
        for _ in range(PEER_TOPK):
            m = jnp.max(rem, axis=0, keepdims=True)
            eq = rem == m
            tau = jnp.where(cum < PEER_TOPK, m, tau)
            cum = cum + jnp.sum(eq.astype(F32), axis=0, keepdims=True)
            rem = jnp.where(eq, NEG_INF, rem)
        zsum = jnp.sum(jnp.where(cand >= tau, jnp.exp(cand - m_top), 0.0), axis=0, keepdims=True)
        inv_z = 1.0 / zsum

        for v in range(n_groups):
            rows = pl.ds(SUBLANES * v, SUBLANES)
            s1g = s_scr[2 * hd, rows, :]
            s2g = s_scr[2 * hd + 1, rows, :]
            need = tau - s1g
            rank = jnp.full((SUBLANES, tr), float(PEER_TOPK), F32)
            count = jnp.full((SUBLANES, tr), float(PEER_TOPK), F32)
            for k in range(PEER_TOPK - 1, -1, -1):
                rank = jnp.where(s2g >= sv2[k], float(k), rank)
                count = jnp.where(sv2[k] < need, float(k), count)
            n_ref[hd, rows, :] = count
            d_ref[hd, rows, :] = jnp.exp(s1g - m1) * inv_z
            s_scr[2 * hd, rows, :] = rank
            s_scr[2 * hd + 1, rows, :] = jnp.exp(s2g - m2)
        r_ref[hd] = s_scr[2 * hd].astype(BF16)
        e2_ref[hd] = s_scr[2 * hd + 1].astype(BF16)
        return carry

    lax.fori_loop(0, PEER_HEADS, per_head, 0)


def _route(h, g, wq, keys):
    t, d = h.shape
    tr = min(ROUTE_TOKENS, t)
    nk = keys.shape[1]
    tab_spec = pl.BlockSpec((PEER_HEADS, nk, tr), lambda i: (0, 0, i))
    return pl.pallas_call(
        _route_kernel,
        out_shape=(
            jax.ShapeDtypeStruct((t, d), BF16),
            jax.ShapeDtypeStruct((PEER_HEADS, nk, t), BF16),
            jax.ShapeDtypeStruct((PEER_HEADS, nk, t), BF16),
            jax.ShapeDtypeStruct((PEER_HEADS, nk, t), F32),
            jax.ShapeDtypeStruct((PEER_HEADS, nk, t), F32),
        ),
        grid=(t // tr,),
        in_specs=[pl.BlockSpec((tr, d), lambda i: (i, 0)), _const_spec(g.shape), _const_spec(wq.shape),
                  _const_spec(keys.shape)],
        out_specs=(pl.BlockSpec((tr, d), lambda i: (i, 0)), tab_spec, tab_spec, tab_spec, tab_spec),
        scratch_shapes=[pltpu.VMEM((2 * PEER_HEADS, nk, tr), F32)],
        compiler_params=_params(("arbitrary",)),
        name="route",
    )(h, g, wq, keys)


PEER_GROUP = 2 * SUBLANES
PEER_LANE_CHUNK = 256


def _peer_kernel(xn_ref, u_ref, vt_ref, r_ref, e2_ref, n_ref, d_ref, o_ref, a_scr, p_scr, acc_ref,
                 *, nkeys, n_exp_tiles):
    s = pl.program_id(0)
    te, tt = a_scr.shape[1], a_scr.shape[2]
    blocks_per_tile = te // nkeys

    @pl.when(s == 0)
    def _():
        a_scr[...] = jnp.zeros_like(a_scr)
        p_scr[...] = jnp.zeros_like(p_scr)
        acc_ref[...] = jnp.zeros_like(acc_ref)

    j3 = lax.rem(s + n_exp_tiles - 2, n_exp_tiles)

    @pl.when(j3 == 0)
    def _():
        acc_ref[...] = jnp.zeros_like(acc_ref)

    cur = lax.rem(s, 2)
    prev = 1 - cur

    a_scr[cur] = _dot_nt(u_ref[...], xn_ref[...])

    j2 = lax.rem(s + n_exp_tiles - 1, n_exp_tiles)
    for ii in range(blocks_per_tile):
        i = j2 * blocks_per_tile + ii
        for lc in range(tt // PEER_LANE_CHUNK):
            lanes = pl.ds(lc * PEER_LANE_CHUNK, PEER_LANE_CHUNK)
            w = [None] * (nkeys // PEER_GROUP)
            for hd in range(PEER_HEADS):
                n_b = jnp.broadcast_to(n_ref[hd, pl.ds(i, 1), lanes], (PEER_GROUP, PEER_LANE_CHUNK)).astype(BF16)
                d_b = jnp.broadcast_to(d_ref[hd, pl.ds(i, 1), lanes], (PEER_GROUP, PEER_LANE_CHUNK)).astype(BF16)
                for g in range(nkeys // PEER_GROUP):
                    rows = pl.ds(g * PEER_GROUP, PEER_GROUP)
                    contrib = jnp.where(r_ref[hd, rows, lanes] < n_b, e2_ref[hd, rows, lanes] * d_b,
                                        jnp.zeros((), BF16))
                    w[g] = contrib if w[g] is None else w[g] + contrib
            for g in range(nkeys // PEER_GROUP):
                rows = pl.ds(ii * nkeys + g * PEER_GROUP, PEER_GROUP)
                act = jax.nn.gelu(a_scr[prev, rows, lanes]).astype(BF16)
                p_scr[prev, rows, lanes] = w[g] * act

    acc_ref[...] += _dot(vt_ref[...], p_scr[cur])

    @pl.when(j3 == n_exp_tiles - 1)
    def _():
        o_ref[...] = acc_ref[...].T


def _peer(xn, u, vt, r_tab, e2_tab, n_tab, d_tab):
    t, d = xn.shape
    n_exp = u.shape[0]
    nk = r_tab.shape[1]
    tt = min(PEER_TOKENS, t)
    te = PEER_EXPERTS_PER_STEP
    nj = n_exp // te
    nt = t // tt
    steps = nt * nj + 2

    def tile_of(pos):
        return jnp.minimum(jnp.maximum(pos, 0) // nj, nt - 1)

    tab_spec = pl.BlockSpec((PEER_HEADS, nk, tt), lambda s: (0, 0, tile_of(s - 1)))
    return pl.pallas_call(
        functools.partial(_peer_kernel, nkeys=nk, n_exp_tiles=nj),
        out_shape=jax.ShapeDtypeStruct((t, d), F32),
        grid=(steps,),
        in_specs=[
            pl.BlockSpec((tt, d), lambda s: (tile_of(s), 0)),
            pl.BlockSpec((te, d), lambda s: (s % nj, 0)),
            pl.BlockSpec((d, te), lambda s: (0, (s + nj - 2) % nj)),
            tab_spec, tab_spec, tab_spec, tab_spec,
        ],
        out_specs=pl.BlockSpec((tt, d), lambda s: (tile_of(s - 2), 0)),
        scratch_shapes=[
            pltpu.VMEM((2, te, tt), F32),
            pltpu.VMEM((2, te, tt), BF16),
            pltpu.VMEM((d, tt), F32),
        ],
        compiler_params=_params(("arbitrary",)),
        name="peer",
    )(xn, u, vt, r_tab, e2_tab, n_tab, d_tab)


def _ple_kernel(h_ref, y_ref, p_ref, g_ref, wg_ref, wp_ref, fg_ref, o_ref, *, final_norm):
    h = h_ref[...] + y_ref[...]
    gate = jax.nn.sigmoid(_dot(_rmsnorm(h, g_ref[...]).astype(BF16), wg_ref[...]))
    out = h + gate * _dot(p_ref[...].astype(BF16), wp_ref[...])
    if final_norm:
        out = _rmsnorm(out, fg_ref[...])
    o_ref[...] = out


def _ple(h, y, p, g, wg, wp, final_g, final_norm):
    t, d = h.shape
    tm = min(PLE_TOKENS, t)
    return pl.pallas_call(
        functools.partial(_ple_kernel, final_norm=final_norm),
        out_shape=jax.ShapeDtypeStruct((t, d), F32),
        grid=(t // tm,),
        in_specs=[pl.BlockSpec((tm, d), lambda i: (i, 0)), pl.BlockSpec((tm, d), lambda i: (i, 0)),
                  pl.BlockSpec((tm, p.shape[1]), lambda i: (i, 0)),
                  _const_spec(g.shape), _const_spec(wg.shape), _const_spec(wp.shape), _const_spec(final_g.shape)],
        out_specs=pl.BlockSpec((tm, d), lambda i: (i, 0)),
        compiler_params=_params(("arbitrary",)),
        name="ple",
    )(h, y, p, g, wg, wp, final_g)


def _block_diag(w, group):
    heads, n, _ = w.shape
    w = w.reshape(heads // group, group, n, n)
    eye = jnp.eye(group, dtype=w.dtype)
    return jnp.einsum("bgij,gk->bgikj", w, eye).reshape(heads // group, group * n, group * n)


def kernel(x, p, norm_mix_g, w_in, conv_w, conv_b, lru_wa, lru_ba, lru_wx, lru_bx, lru_lam, pool_w, pool_scale,
           w_out, norm_ffn_g, peer_wq, peer_keys, peer_u, peer_v, norm_ple_g, ple_wp, ple_wg, final_g):
    batch, seq, d = x.shape
    depth = w_in.shape[0]
    lru_hd = lru_wa.shape[-1]
    row = lambda v: v.reshape(1, -1)
    outs = []
    for b in range(batch):
        h = x[b]
        for i in range(depth):
            group = MXU_DIM // lru_hd
            h = _mix(h, row(norm_mix_g[i]), w_in[i].astype(BF16), conv_w[i], row(conv_b[i]),
                     _block_diag(lru_wa[i], group).astype(BF16), row(lru_ba[i]),
                     _block_diag(lru_wx[i], group).astype(BF16), row(lru_bx[i]), row(lru_lam[i]),
                     pool_w[i].astype(BF16), row(pool_scale[i]), w_out[i].astype(BF16))
            xn, r_tab, e2_tab, n_tab, d_tab = _route(h, row(norm_ffn_g[i]), peer_wq[i].astype(BF16),
                                                     peer_keys[i].astype(BF16))
            y = _peer(xn, peer_u[i].astype(BF16), peer_v[i].T.astype(BF16), r_tab, e2_tab, n_tab, d_tab)
            h = _ple(h, y, p[i, b], row(norm_ple_g[i]), ple_wg[i].astype(BF16), ple_wp[i].astype(BF16),
                     row(final_g), final_norm=(i == depth - 1))
        outs.append(h)
    return jnp.stack(outs, axis=0)
```

```python
import functools
import math

import jax
import jax.numpy as jnp
from jax import lax
from jax.experimental import pallas as pl
from jax.experimental.pallas import tpu as pltpu

F32 = jnp.float32
BF16 = jnp.bfloat16

CONV_WIDTH = 4
LRU_HEADS = 16
LRU_C = 8.0
POOL_WINDOWS = (2, 4, 8, 16)
PEER_HEADS = 8
PEER_NKEYS = 128
PEER_TOPK = 16
RMS_EPS = 1e-6

SUBLANES = 8
LANES = 128
MXU_DIM = 256
VMEM_LIMIT_BYTES = 56 * 1024 * 1024

MIX_TOKENS = 256
ROUTE_TOKENS = 256
PEER_TOKENS = 512
PEER_EXPERTS_PER_STEP = 1024
PLE_TOKENS = 512

CONV_HALO = SUBLANES
POOL_HALO = 2 * SUBLANES
NEG_INF = float("-inf")


def _rmsnorm(x, g):
    return x * lax.rsqrt(jnp.mean(x * x, axis=-1, keepdims=True) + RMS_EPS) * g


def _dot(a, b):
    return jnp.dot(a, b, preferred_element_type=F32)


def _dot_nt(a, b):
    return lax.dot_general(a, b, (((1,), (1,)), ((), ())), preferred_element_type=F32)


def _layer_spec(arr, layer):
    index = (layer,) + (0,) * (arr.ndim - 1)
    return pl.BlockSpec((None,) + arr.shape[1:], lambda *_: index, pipeline_mode=pl.Buffered(1))


def _params(semantics):
    return pltpu.CompilerParams(dimension_semantics=semantics, vmem_limit_bytes=VMEM_LIMIT_BYTES)


def _mix_kernel(h_ref, g_ref, win_ref, convw_ref, convb_ref, wa_ref, ba_ref, wx_ref, bx_ref, lam_ref,
                poolw_ref, pscale_ref, wout_ref, o_ref,
                xbuf, pbuf, hstate, cat_ref, *, d_lru, d_pool):
    tm = h_ref.shape[0]
    step = pl.program_id(0)

    @pl.when(step == 0)
    def _():
        xbuf[0:CONV_HALO, :] = jnp.zeros((CONV_HALO, d_lru), F32)
        pbuf[0:POOL_HALO, :] = jnp.zeros((POOL_HALO, d_pool), F32)
        hstate[...] = jnp.zeros_like(hstate)

    h = h_ref[...]
    hn = _rmsnorm(h, g_ref[...]).astype(BF16)
    z = _dot(hn, win_ref[...])
    xbuf[CONV_HALO:CONV_HALO + tm, :] = z[:, :d_lru]
    y_gate = z[:, d_lru:2 * d_lru]
    pbuf[POOL_HALO:POOL_HALO + tm, :] = z[:, 2 * d_lru:]

    x = convb_ref[...] + xbuf[CONV_HALO:CONV_HALO + tm, :] * convw_ref[CONV_WIDTH - 1:CONV_WIDTH, :]
    for k in range(1, CONV_WIDTH):
        x = x + xbuf[CONV_HALO - k:CONV_HALO - k + tm, :] * convw_ref[CONV_WIDTH - 1 - k:CONV_WIDTH - k, :]
    xbuf[0:CONV_HALO, :] = xbuf[tm:tm + CONV_HALO, :]

    xb = x.astype(BF16)
    n_blk = d_lru // MXU_DIM
    r_parts, i_parts = [], []
    for b in range(n_blk):
        xs = xb[:, b * MXU_DIM:(b + 1) * MXU_DIM]
        r_parts.append(_dot(xs, wa_ref[b]))
        i_parts.append(_dot(xs, wx_ref[b]))
    r = jax.nn.sigmoid(jnp.concatenate(r_parts, axis=1) + ba_ref[...])
    ig = jax.nn.sigmoid(jnp.concatenate(i_parts, axis=1) + bx_ref[...])
    lam = lam_ref[...]
    softplus_neg_lam = jnp.maximum(-lam, 0.0) + jnp.log(1.0 + jnp.exp(-jnp.abs(lam)))
    log_a = (-LRU_C) * r * softplus_neg_lam
    a = jnp.exp(log_a)
    mult = jnp.sqrt(1.0 - jnp.exp(2.0 * log_a))
    b_in = mult * ig * x

    row = lax.broadcasted_iota(jnp.int32, (tm, d_lru), 0)
    shift = 1
    while shift < tm:
        if shift < SUBLANES:
            keep = row >= shift
            a_prev = jnp.where(keep, pltpu.roll(a, shift, axis=0), 1.0)
            b_prev = jnp.where(keep, pltpu.roll(b_in, shift, axis=0), 0.0)
        else:
            a_prev = jnp.concatenate([jnp.ones((shift, d_lru), F32), a[:tm - shift]], axis=0)
            b_prev = jnp.concatenate([jnp.zeros((shift, d_lru), F32), b_in[:tm - shift]], axis=0)
        b_in = a * b_prev + b_in
        a = a * a_prev
        shift *= 2
    hs = a * hstate[...] + b_in
    hstate[...] = hs[tm - 1:tm, :]
    cat_ref[:, 0:d_lru] = (hs * jax.nn.gelu(y_gate)).astype(BF16)

    n_groups = len(POOL_WINDOWS)
    gdim = d_pool // n_groups
    frames = step * tm + 1 + lax.broadcasted_iota(jnp.int32, (tm, gdim), 0)
    for gi, win in enumerate(POOL_WINDOWS):
        cols = slice(gi * gdim, (gi + 1) * gdim)
        cur = pbuf[POOL_HALO:POOL_HALO + tm, cols]
        s = cur
        for k in range(1, win):
            s = s + pbuf[POOL_HALO - k:POOL_HALO - k + tm, cols]
        count = jnp.minimum(frames, win).astype(F32)
        d = (s / count - cur).astype(BF16)
        y = _dot(d, poolw_ref[gi]) * pscale_ref[:, cols]
        cat_ref[:, d_lru + gi * gdim:d_lru + (gi + 1) * gdim] = y.astype(BF16)
    pbuf[0:POOL_HALO, :] = pbuf[tm:tm + POOL_HALO, :]

    o_ref[...] = h + _dot(cat_ref[...], wout_ref[...])


def _mix(layer, h, g, w_in, conv_w, conv_b, wa_bd, ba, wx_bd, bx, lam, pool_w, pool_scale, w_out):
    t, d = h.shape
    d_lru = conv_w.shape[-1]
    d_pool = pool_scale.shape[-1]
    tm = min(MIX_TOKENS, t)
    row_spec = pl.BlockSpec((tm, d), lambda i: (i, 0))
    args = (g, w_in, conv_w, conv_b, wa_bd, ba, wx_bd, bx, lam, pool_w, pool_scale, w_out)
    return pl.pallas_call(
        functools.partial(_mix_kernel, d_lru=d_lru, d_pool=d_pool),
        out_shape=jax.ShapeDtypeStruct((t, d), F32),
        grid=(t // tm,),
        in_specs=[row_spec] + [_layer_spec(a, layer) for a in args],
        out_specs=row_spec,
        scratch_shapes=[
            pltpu.VMEM((tm + CONV_HALO, d_lru), F32),
            pltpu.VMEM((tm + POOL_HALO, d_pool), F32),
            pltpu.VMEM((1, d_lru), F32),
            pltpu.VMEM((tm, d_lru + d_pool), BF16),
        ],
        compiler_params=_params(("arbitrary",)),
        name="mix",
    )(h, *args)


def _sort16_network():
    pairs = []

    def merge(lo, n, r):
        step = r * 2
        if step < n:
            merge(lo, n, step)
            merge(lo + r, n, step)
            for i in range(lo + r, lo + n - r, step):
                pairs.append((i, i + r))
        else:
            pairs.append((lo, lo + r))

    def sort(lo, n):
        if n > 1:
            m = n // 2
            sort(lo, m)
            sort(lo + m, m)
            merge(lo, n, 1)

    sort(0, PEER_TOPK)
    return tuple(pairs)


_SORT16 = _sort16_network()


def _sort_desc(vals):
    vals = list(vals)
    for i, j in _SORT16:
        hi = jnp.maximum(vals[i], vals[j])
        lo = jnp.minimum(vals[i], vals[j])
        vals[i], vals[j] = hi, lo
    return vals


def _bitonic_desc(vals):
    vals = list(vals)
    n = len(vals)
    gap = n // 2
    while gap >= 1:
        for i in range(n):
            if (i // gap) % 2 == 0:
                hi = jnp.maximum(vals[i], vals[i + gap])
                lo = jnp.minimum(vals[i], vals[i + gap])
                vals[i], vals[i + gap] = hi, lo
        gap //= 2
    return vals


def _top16_rows(s_ref, idx):
    groups = [s_ref[idx, pl.ds(SUBLANES * v, SUBLANES), :] for v in range(PEER_NKEYS // SUBLANES)]
    vals = _sort_desc(groups)
    shift = SUBLANES // 2
    while shift >= 1:
        other = [pltpu.roll(v, shift, axis=0) for v in vals]
        merged = [jnp.maximum(vals[k], other[PEER_TOPK - 1 - k]) for k in range(PEER_TOPK)]
        vals = _bitonic_desc(merged)
        shift //= 2
    return vals


def _pack_rows(vals):
    sub = lax.broadcasted_iota(jnp.int32, vals[0].shape, 0)
    out = vals[SUBLANES - 1]
    for k in range(SUBLANES - 2, -1, -1):
        out = jnp.where(sub == k, vals[k], out)
    return out


def _route_kernel(h_ref, g_ref, wq_ref, keys_ref, xn_ref, r_ref, e2_ref, n_ref, d_ref, s_scr):
    tr = h_ref.shape[0]
    xn = _rmsnorm(h_ref[...], g_ref[...]).astype(BF16)
    xn_ref[...] = xn
    q = _dot(xn, wq_ref[...]).astype(BF16)
    half = keys_ref.shape[2]
    for hd in range(PEER_HEADS):
        for c in range(2):
            col = (hd * 2 + c) * half
            s_scr[hd * 2 + c] = _dot_nt(keys_ref[c], q[:, col:col + half])

    n_groups = PEER_NKEYS // SUBLANES

    def per_head(hd, carry):
        sv1 = _top16_rows(s_scr, 2 * hd)
        sv2 = _top16_rows(s_scr, 2 * hd + 1)
        p1_hi = _pack_rows(sv1[SUBLANES:])
        p2_lo = _pack_rows(sv2[:SUBLANES])
        p2_hi = _pack_rows(sv2[SUBLANES:])
        cands = [sv1[0] + p2_lo, sv1[0] + p2_hi, p1_hi + sv2[0]]
        cands += [sv1[a] + p2_lo for a in range(1, SUBLANES)]
        cand = jnp.concatenate(cands, axis=0)
        m1, m2 = sv1[0], sv2[0]
        m_top = m1[0:1, :] + m2[0:1, :]
        rem = cand
        tau = jnp.full((1, tr), NEG_INF, F32)
        cum = jnp.zeros((1, tr), F32)
        for _ in range(PEER_TOPK):
            m = jnp.max(rem, axis=0, keepdims=True)
            eq = rem == m
            tau = jnp.where(cum < PEER_TOPK, m, tau)
            cum = cum + jnp.sum(eq.astype(F32), axis=0, keepdims=True)
            rem = jnp.where(eq, NEG_INF, rem)
        zsum = jnp.sum(jnp.where(cand >= tau, jnp.exp(cand - m_top), 0.0), axis=0, keepdims=True)
        inv_z = 1.0 / zsum

        for v in range(n_groups):
            rows = pl.ds(SUBLANES * v, SUBLANES)
            s1g = s_scr[2 * hd, rows, :]
            s2g = s_scr[2 * hd + 1, rows, :]
            rank = jnp.full((SUBLANES, tr), float(PEER_TOPK), F32)
            count = jnp.full((SUBLANES, tr), float(PEER_TOPK), F32)
            for k in range(PEER_TOPK - 1, -1, -1):
                rank = jnp.where(s2g >= sv2[k], float(k), rank)
                count = jnp.where(s1g + sv2[k] < tau, float(k), count)
            n_ref[hd, rows, :] = count
            d_ref[hd, rows, :] = jnp.exp(s1g - m1) * inv_z
            s_scr[2 * hd, rows, :] = rank
            s_scr[2 * hd + 1, rows, :] = jnp.exp(s2g - m2)
        r_ref[hd] = s_scr[2 * hd].astype(BF16)
        e2_ref[hd] = s_scr[2 * hd + 1].astype(BF16)
        return carry

    lax.fori_loop(0, PEER_HEADS, per_head, 0)


def _route(layer, h, g, wq, keys):
    t, d = h.shape
    tr = min(ROUTE_TOKENS, t)
    nk = keys.shape[2]
    tab_spec = pl.BlockSpec((PEER_HEADS, nk, tr), lambda i: (0, 0, i))
    return pl.pallas_call(
        _route_kernel,
        out_shape=(
            jax.ShapeDtypeStruct((t, d), BF16),
            jax.ShapeDtypeStruct((PEER_HEADS, nk, t), BF16),
            jax.ShapeDtypeStruct((PEER_HEADS, nk, t), BF16),
            jax.ShapeDtypeStruct((PEER_HEADS, nk, t), F32),
            jax.ShapeDtypeStruct((PEER_HEADS, nk, t), F32),
        ),
        grid=(t // tr,),
        in_specs=[pl.BlockSpec((tr, d), lambda i: (i, 0)), _layer_spec(g, layer), _layer_spec(wq, layer),
                  _layer_spec(keys, layer)],
        out_specs=(pl.BlockSpec((tr, d), lambda i: (i, 0)), tab_spec, tab_spec, tab_spec, tab_spec),
        scratch_shapes=[pltpu.VMEM((2 * PEER_HEADS, nk, tr), F32)],
        compiler_params=_params(("arbitrary",)),
        name="route",
    )(h, g, wq, keys)


PEER_GROUP = 2 * SUBLANES
PEER_LANE_CHUNK = 256


def _peer_kernel(xn_ref, u_ref, vt_ref, r_ref, e2_ref, n_ref, d_ref, o_ref, a_scr, p_scr, acc_ref,
                 *, nkeys, n_exp_tiles):
    s = pl.program_id(0)
    te, tt = a_scr.shape[1], a_scr.shape[2]
    blocks_per_tile = te // nkeys

    @pl.when(s == 0)
    def _():
        a_scr[...] = jnp.zeros_like(a_scr)
        p_scr[...] = jnp.zeros_like(p_scr)
        acc_ref[...] = jnp.zeros_like(acc_ref)

    j3 = lax.rem(s + n_exp_tiles - 2, n_exp_tiles)

    @pl.when(j3 == 0)
    def _():
        acc_ref[...] = jnp.zeros_like(acc_ref)

    cur = lax.rem(s, 2)
    prev = 1 - cur

    a_scr[cur] = jax.nn.gelu(_dot_nt(u_ref[...], xn_ref[...])).astype(BF16)

    j2 = lax.rem(s + n_exp_tiles - 1, n_exp_tiles)
    for ii in range(blocks_per_tile):
        i = j2 * blocks_per_tile + ii
        for lc in range(tt // PEER_LANE_CHUNK):
            lanes = pl.ds(lc * PEER_LANE_CHUNK, PEER_LANE_CHUNK)
            w = [None] * (nkeys // PEER_GROUP)
            for hd in range(PEER_HEADS):
                n_b = jnp.broadcast_to(n_ref[hd, pl.ds(i, 1), lanes], (PEER_GROUP, PEER_LANE_CHUNK)).astype(BF16)
                d_b = jnp.broadcast_to(d_ref[hd, pl.ds(i, 1), lanes], (PEER_GROUP, PEER_LANE_CHUNK)).astype(BF16)
                for g in range(nkeys // PEER_GROUP):
                    rows = pl.ds(g * PEER_GROUP, PEER_GROUP)
                    contrib = jnp.where(r_ref[hd, rows, lanes] < n_b, e2_ref[hd, rows, lanes] * d_b,
                                        jnp.zeros((), BF16))
                    w[g] = contrib if w[g] is None else w[g] + contrib
            for g in range(nkeys // PEER_GROUP):
                rows = pl.ds(ii * nkeys + g * PEER_GROUP, PEER_GROUP)
                p_scr[prev, rows, lanes] = w[g] * a_scr[prev, rows, lanes]

    acc_ref[...] += _dot(vt_ref[...], p_scr[cur])

    @pl.when(j3 == n_exp_tiles - 1)
    def _():
        o_ref[...] = acc_ref[...].T


def _peer(layer, xn, u, vt, r_tab, e2_tab, n_tab, d_tab):
    t, d = xn.shape
    n_exp = u.shape[1]
    nk = r_tab.shape[1]
    tt = min(PEER_TOKENS, t)
    te = PEER_EXPERTS_PER_STEP
    nj = n_exp // te
    nt = t // tt
    steps = nt * nj + 2

    def tile_of(pos):
        return jnp.minimum(jnp.maximum(pos, 0) // nj, nt - 1)

    tab_spec = pl.BlockSpec((PEER_HEADS, nk, tt), lambda s: (0, 0, tile_of(s - 1)))
    return pl.pallas_call(
        functools.partial(_peer_kernel, nkeys=nk, n_exp_tiles=nj),
        out_shape=jax.ShapeDtypeStruct((t, d), F32),
        grid=(steps,),
        in_specs=[
            pl.BlockSpec((tt, d), lambda s: (tile_of(s), 0)),
            pl.BlockSpec((None, te, d), lambda s: (layer, s % nj, 0)),
            pl.BlockSpec((None, d, te), lambda s: (layer, 0, (s + nj - 2) % nj)),
            tab_spec, tab_spec, tab_spec, tab_spec,
        ],
        out_specs=pl.BlockSpec((tt, d), lambda s: (tile_of(s - 2), 0)),
        scratch_shapes=[
            pltpu.VMEM((2, te, tt), BF16),
            pltpu.VMEM((2, te, tt), BF16),
            pltpu.VMEM((d, tt), F32),
        ],
        compiler_params=_params(("arbitrary",)),
        name="peer",
    )(xn, u, vt, r_tab, e2_tab, n_tab, d_tab)


def _ple_kernel(h_ref, y_ref, p_ref, g_ref, wg_ref, wp_ref, fg_ref, o_ref, *, final_norm):
    h = h_ref[...] + y_ref[...]
    gate = jax.nn.sigmoid(_dot(_rmsnorm(h, g_ref[...]).astype(BF16), wg_ref[...]))
    out = h + gate * _dot(p_ref[...].astype(BF16), wp_ref[...])
    if final_norm:
        out = _rmsnorm(out, fg_ref[...])
    o_ref[...] = out


def _ple(layer, batch, h, y, p, g, wg, wp, final_g, final_norm):
    t, d = h.shape
    tm = min(PLE_TOKENS, t)
    final_spec = pl.BlockSpec(final_g.shape, lambda i: (0, 0), pipeline_mode=pl.Buffered(1))
    return pl.pallas_call(
        functools.partial(_ple_kernel, final_norm=final_norm),
        out_shape=jax.ShapeDtypeStruct((t, d), F32),
        grid=(t // tm,),
        in_specs=[pl.BlockSpec((tm, d), lambda i: (i, 0)), pl.BlockSpec((tm, d), lambda i: (i, 0)),
                  pl.BlockSpec((None, None, tm, p.shape[-1]), lambda i: (layer, batch, i, 0)),
                  _layer_spec(g, layer), _layer_spec(wg, layer), _layer_spec(wp, layer), final_spec],
        out_specs=pl.BlockSpec((tm, d), lambda i: (i, 0)),
        compiler_params=_params(("arbitrary",)),
        name="ple",
    )(h, y, p, g, wg, wp, final_g)


def _block_diag(w, group):
    layers, heads, n, _ = w.shape
    w = w.reshape(layers, heads // group, group, n, n)
    eye = jnp.eye(group, dtype=w.dtype)
    return jnp.einsum("lbgij,gk->lbgikj", w, eye).reshape(layers, heads // group, group * n, group * n)


def kernel(x, p, norm_mix_g, w_in, conv_w, conv_b, lru_wa, lru_ba, lru_wx, lru_bx, lru_lam, pool_w, pool_scale,
           w_out, norm_ffn_g, peer_wq, peer_keys, peer_u, peer_v, norm_ple_g, ple_wp, ple_wg, final_g):
    batch, seq, d = x.shape
    depth = w_in.shape[0]
    group = MXU_DIM // lru_wa.shape[-1]
    rows = lambda v: v.reshape(depth, 1, -1)
    bf = lambda v: v.astype(BF16)
    mix_args = (rows(norm_mix_g), bf(w_in), conv_w, rows(conv_b), bf(_block_diag(lru_wa, group)), rows(lru_ba),
                bf(_block_diag(lru_wx, group)), rows(lru_bx), rows(lru_lam), bf(pool_w), rows(pool_scale), bf(w_out))
    route_args = (rows(norm_ffn_g), bf(peer_wq), bf(peer_keys))
    u_all = bf(peer_u)
    vt_all = bf(jnp.swapaxes(peer_v, 1, 2))
    ple_args = (rows(norm_ple_g), bf(ple_wg), bf(ple_wp), final_g.reshape(1, -1))
    outs = []
    for b in range(batch):
        h = x[b]
        for i in range(depth):
            h = _mix(i, h, *mix_args)
            xn, r_tab, e2_tab, n_tab, d_tab = _route(i, h, *route_args)
            y = _peer(i, xn, u_all, vt_all, r_tab, e2_tab, n_tab, d_tab)
            h = _ple(i, b, h, y, p, *ple_args, final_norm=(i == depth - 1))
        outs.append(h)
    return jnp.stack(outs, axis=0)
```

```python
import functools
import math

import jax
import jax.numpy as jnp
from jax import lax
from jax.experimental import pallas as pl
from jax.experimental.pallas import tpu as pltpu

F32 = jnp.float32
BF16 = jnp.bfloat16

CONV_WIDTH = 4
LRU_HEADS = 16
LRU_C = 8.0
POOL_WINDOWS = (2, 4, 8, 16)
PEER_HEADS = 8
PEER_NKEYS = 128
PEER_TOPK = 16
RMS_EPS = 1e-6

SUBLANES = 8
LANES = 128
MXU_DIM = 256
VMEM_LIMIT_BYTES = 56 * 1024 * 1024

MIX_TOKENS = 256
ROUTE_TOKENS = 256
PEER_TOKENS = 512
PEER_EXPERTS_PER_STEP = 1024
PLE_TOKENS = 512

CONV_HALO = SUBLANES
POOL_HALO = 2 * SUBLANES
NEG_INF = float("-inf")


def _rmsnorm(x, g):
    return x * lax.rsqrt(jnp.mean(x * x, axis=-1, keepdims=True) + RMS_EPS) * g


def _dot(a, b):
    return jnp.dot(a, b, preferred_element_type=F32)


def _dot_nt(a, b):
    return lax.dot_general(a, b, (((1,), (1,)), ((), ())), preferred_element_type=F32)


def _layer_spec(arr, layer):
    index = (layer,) + (0,) * (arr.ndim - 1)
    return pl.BlockSpec((None,) + arr.shape[1:], lambda *_: index, pipeline_mode=pl.Buffered(1))


def _params(semantics):
    return pltpu.CompilerParams(dimension_semantics=semantics, vmem_limit_bytes=VMEM_LIMIT_BYTES)


def _mix_kernel(h_ref, g_ref, win_ref, convw_ref, convb_ref, wa_ref, ba_ref, wx_ref, bx_ref, lam_ref,
                poolw_ref, pscale_ref, wout_ref, o_ref,
                xbuf, pbuf, hstate, cat_ref, *, d_lru, d_pool):
    tm = h_ref.shape[0]
    step = pl.program_id(0)

    @pl.when(step == 0)
    def _():
        xbuf[0:CONV_HALO, :] = jnp.zeros((CONV_HALO, d_lru), F32)
        pbuf[0:POOL_HALO, :] = jnp.zeros((POOL_HALO, d_pool), F32)
        hstate[...] = jnp.zeros_like(hstate)

    h = h_ref[...]
    hn = _rmsnorm(h, g_ref[...]).astype(BF16)
    z = _dot(hn, win_ref[...])
    xbuf[CONV_HALO:CONV_HALO + tm, :] = z[:, :d_lru]
    y_gate = z[:, d_lru:2 * d_lru]
    pbuf[POOL_HALO:POOL_HALO + tm, :] = z[:, 2 * d_lru:]

    x = convb_ref[...] + xbuf[CONV_HALO:CONV_HALO + tm, :] * convw_ref[CONV_WIDTH - 1:CONV_WIDTH, :]
    for k in range(1, CONV_WIDTH):
        x = x + xbuf[CONV_HALO - k:CONV_HALO - k + tm, :] * convw_ref[CONV_WIDTH - 1 - k:CONV_WIDTH - k, :]
    xbuf[0:CONV_HALO, :] = xbuf[tm:tm + CONV_HALO, :]

    xb = x.astype(BF16)
    n_blk = d_lru // MXU_DIM
    r_parts, i_parts = [], []
    for b in range(n_blk):
        xs = xb[:, b * MXU_DIM:(b + 1) * MXU_DIM]
        r_parts.append(_dot(xs, wa_ref[b]))
        i_parts.append(_dot(xs, wx_ref[b]))
    r = jax.nn.sigmoid(jnp.concatenate(r_parts, axis=1) + ba_ref[...])
    ig = jax.nn.sigmoid(jnp.concatenate(i_parts, axis=1) + bx_ref[...])
    lam = lam_ref[...]
    softplus_neg_lam = jnp.maximum(-lam, 0.0) + jnp.log(1.0 + jnp.exp(-jnp.abs(lam)))
    log_a = (-LRU_C) * r * softplus_neg_lam
    a = jnp.exp(log_a)
    mult = jnp.sqrt(1.0 - jnp.exp(2.0 * log_a))
    b_in = mult * ig * x

    row = lax.broadcasted_iota(jnp.int32, (tm, d_lru), 0)
    shift = 1
    while shift < tm:
        if shift < SUBLANES:
            keep = row >= shift
            a_prev = jnp.where(keep, pltpu.roll(a, shift, axis=0), 1.0)
            b_prev = jnp.where(keep, pltpu.roll(b_in, shift, axis=0), 0.0)
        else:
            a_prev = jnp.concatenate([jnp.ones((shift, d_lru), F32), a[:tm - shift]], axis=0)
            b_prev = jnp.concatenate([jnp.zeros((shift, d_lru), F32), b_in[:tm - shift]], axis=0)
        b_in = a * b_prev + b_in
        a = a * a_prev
        shift *= 2
    hs = a * hstate[...] + b_in
    hstate[...] = hs[tm - 1:tm, :]
    cat_ref[:, 0:d_lru] = (hs * jax.nn.gelu(y_gate)).astype(BF16)

    n_groups = len(POOL_WINDOWS)
    gdim = d_pool // n_groups
    frames = step * tm + 1 + lax.broadcasted_iota(jnp.int32, (tm, gdim), 0)
    for gi, win in enumerate(POOL_WINDOWS):
        cols = slice(gi * gdim, (gi + 1) * gdim)
        cur = pbuf[POOL_HALO:POOL_HALO + tm, cols]
        s = cur
        for k in range(1, win):
            s = s + pbuf[POOL_HALO - k:POOL_HALO - k + tm, cols]
        count = jnp.minimum(frames, win).astype(F32)
        d = (s / count - cur).astype(BF16)
        y = _dot(d, poolw_ref[gi]) * pscale_ref[:, cols]
        cat_ref[:, d_lru + gi * gdim:d_lru + (gi + 1) * gdim] = y.astype(BF16)
    pbuf[0:POOL_HALO, :] = pbuf[tm:tm + POOL_HALO, :]

    o_ref[...] = h + _dot(cat_ref[...], wout_ref[...])


def _mix(layer, h, g, w_in, conv_w, conv_b, wa_bd, ba, wx_bd, bx, lam, pool_w, pool_scale, w_out):
    t, d = h.shape
    d_lru = conv_w.shape[-1]
    d_pool = pool_scale.shape[-1]
    tm = min(MIX_TOKENS, t)
    row_spec = pl.BlockSpec((tm, d), lambda i: (i, 0))
    args = (g, w_in, conv_w, conv_b, wa_bd, ba, wx_bd, bx, lam, pool_w, pool_scale, w_out)
    return pl.pallas_call(
        functools.partial(_mix_kernel, d_lru=d_lru, d_pool=d_pool),
        out_shape=jax.ShapeDtypeStruct((t, d), F32),
        grid=(t // tm,),
        in_specs=[row_spec] + [_layer_spec(a, layer) for a in args],
        out_specs=row_spec,
        scratch_shapes=[
            pltpu.VMEM((tm + CONV_HALO, d_lru), F32),
            pltpu.VMEM((tm + POOL_HALO, d_pool), F32),
            pltpu.VMEM((1, d_lru), F32),
            pltpu.VMEM((tm, d_lru + d_pool), BF16),
        ],
        compiler_params=_params(("arbitrary",)),
        name="mix",
    )(h, *args)


def _sort16_network():
    pairs = []

    def merge(lo, n, r):
        step = r * 2
        if step < n:
            merge(lo, n, step)
            merge(lo + r, n, step)
            for i in range(lo + r, lo + n - r, step):
                pairs.append((i, i + r))
        else:
            pairs.append((lo, lo + r))

    def sort(lo, n):
        if n > 1:
            m = n // 2
            sort(lo, m)
            sort(lo + m, m)
            merge(lo, n, 1)

    sort(0, PEER_TOPK)
    return tuple(pairs)


_SORT16 = _sort16_network()


def _sort_desc(vals):
    vals = list(vals)
    for i, j in _SORT16:
        hi = jnp.maximum(vals[i], vals[j])
        lo = jnp.minimum(vals[i], vals[j])
        vals[i], vals[j] = hi, lo
    return vals


def _bitonic_desc(vals):
    vals = list(vals)
    n = len(vals)
    gap = n // 2
    while gap >= 1:
        for i in range(n):
            if (i // gap) % 2 == 0:
                hi = jnp.maximum(vals[i], vals[i + gap])
                lo = jnp.minimum(vals[i], vals[i + gap])
                vals[i], vals[i + gap] = hi, lo
        gap //= 2
    return vals


def _top16_rows(s_ref, idx):
    groups = [s_ref[idx, pl.ds(SUBLANES * v, SUBLANES), :] for v in range(PEER_NKEYS // SUBLANES)]
    vals = _sort_desc(groups)
    shift = SUBLANES // 2
    while shift >= 1:
        other = [pltpu.roll(v, shift, axis=0) for v in vals]
        merged = [jnp.maximum(vals[k], other[PEER_TOPK - 1 - k]) for k in range(PEER_TOPK)]
        vals = _bitonic_desc(merged)
        shift //= 2
    return vals


def _pack_rows(vals):
    sub = lax.broadcasted_iota(jnp.int32, vals[0].shape, 0)
    out = vals[SUBLANES - 1]
    for k in range(SUBLANES - 2, -1, -1):
        out = jnp.where(sub == k, vals[k], out)
    return out


def _route_kernel(h_ref, g_ref, wq_ref, keys_ref, xnt_ref, r_ref, e2_ref, n_ref, d_ref, s_scr):
    tr = h_ref.shape[0]
    xn_f32 = _rmsnorm(h_ref[...], g_ref[...])
    xnt_ref[...] = xn_f32.T.astype(BF16)
    xn = xn_f32.astype(BF16)
    q = _dot(xn, wq_ref[...]).astype(BF16)
    half = keys_ref.shape[2]
    for hd in range(PEER_HEADS):
        for c in range(2):
            col = (hd * 2 + c) * half
            s_scr[hd * 2 + c] = _dot_nt(keys_ref[c], q[:, col:col + half])

    n_groups = PEER_NKEYS // SUBLANES

    def per_head(hd, carry):
        sv1 = _top16_rows(s_scr, 2 * hd)
        sv2 = _top16_rows(s_scr, 2 * hd + 1)
        p1_hi = _pack_rows(sv1[SUBLANES:])
        p2_lo = _pack_rows(sv2[:SUBLANES])
        p2_hi = _pack_rows(sv2[SUBLANES:])
        cands = [sv1[0] + p2_lo, sv1[0] + p2_hi, p1_hi + sv2[0]]
        cands += [sv1[a] + p2_lo for a in range(1, SUBLANES)]
        cand = jnp.concatenate(cands, axis=0)
        m1, m2 = sv1[0], sv2[0]
        m_top = m1[0:1, :] + m2[0:1, :]
        rem = cand
        tau = jnp.full((1, tr), NEG_INF, F32)
        cum = jnp.zeros((1, tr), F32)
        for _ in range(PEER_TOPK):
            m = jnp.max(rem, axis=0, keepdims=True)
            eq = rem == m
            tau = jnp.where(cum < PEER_TOPK, m, tau)
            cum = cum + jnp.sum(eq.astype(F32), axis=0, keepdims=True)
            rem = jnp.where(eq, NEG_INF, rem)
        zsum = jnp.sum(jnp.where(cand >= tau, jnp.exp(cand - m_top), 0.0), axis=0, keepdims=True)
        inv_z = 1.0 / zsum

        for v in range(n_groups):
            rows = pl.ds(SUBLANES * v, SUBLANES)
            s1g = s_scr[2 * hd, rows, :]
            s2g = s_scr[2 * hd + 1, rows, :]
            rank = jnp.full((SUBLANES, tr), float(PEER_TOPK), F32)
            count = jnp.full((SUBLANES, tr), float(PEER_TOPK), F32)
            for k in range(PEER_TOPK - 1, -1, -1):
                rank = jnp.where(s2g >= sv2[k], float(k), rank)
                count = jnp.where(s1g + sv2[k] < tau, float(k), count)
            n_ref[hd, rows, :] = count
            d_ref[hd, rows, :] = jnp.exp(s1g - m1) * inv_z
            s_scr[2 * hd, rows, :] = rank
            s_scr[2 * hd + 1, rows, :] = jnp.exp(s2g - m2)
        r_ref[hd] = s_scr[2 * hd].astype(BF16)
        e2_ref[hd] = s_scr[2 * hd + 1].astype(BF16)
        return carry

    lax.fori_loop(0, PEER_HEADS, per_head, 0)


def _route(layer, h, g, wq, keys):
    t, d = h.shape
    tr = min(ROUTE_TOKENS, t)
    nk = keys.shape[2]
    tab_spec = pl.BlockSpec((PEER_HEADS, nk, tr), lambda i: (0, 0, i))
    return pl.pallas_call(
        _route_kernel,
        out_shape=(
            jax.ShapeDtypeStruct((d, t), BF16),
            jax.ShapeDtypeStruct((PEER_HEADS, nk, t), BF16),
            jax.ShapeDtypeStruct((PEER_HEADS, nk, t), BF16),
            jax.ShapeDtypeStruct((PEER_HEADS, nk, t), F32),
            jax.ShapeDtypeStruct((PEER_HEADS, nk, t), F32),
        ),
        grid=(t // tr,),
        in_specs=[pl.BlockSpec((tr, d), lambda i: (i, 0)), _layer_spec(g, layer), _layer_spec(wq, layer),
                  _layer_spec(keys, layer)],
        out_specs=(pl.BlockSpec((d, tr), lambda i: (0, i)), tab_spec, tab_spec, tab_spec, tab_spec),
        scratch_shapes=[pltpu.VMEM((2 * PEER_HEADS, nk, tr), F32)],
        compiler_params=_params(("arbitrary",)),
        name="route",
    )(h, g, wq, keys)


PEER_GROUP = 2 * SUBLANES
PEER_LANE_CHUNK = 256
PEER_PIECE = MXU_DIM


def _peer_kernel(xnt_ref, u_ref, vt_ref, r_ref, e2_ref, n_ref, d_ref, o_ref, p_scr, acc_ref, *, nkeys):
    j = pl.program_id(1)
    te, tt = p_scr.shape
    blocks_per_piece = PEER_PIECE // nkeys

    @pl.when(j == 0)
    def _():
        acc_ref[...] = jnp.zeros_like(acc_ref)

    for piece in range(te // PEER_PIECE):
        prow = pl.ds(piece * PEER_PIECE, PEER_PIECE)
        act = jax.nn.gelu(_dot(u_ref[prow, :], xnt_ref[...])).astype(BF16)
        for bi in range(blocks_per_piece):
            ii = piece * blocks_per_piece + bi
            for lc in range(tt // PEER_LANE_CHUNK):
                lanes = pl.ds(lc * PEER_LANE_CHUNK, PEER_LANE_CHUNK)
                w = [None] * (nkeys // PEER_GROUP)
                for hd in range(PEER_HEADS):
                    n_b = jnp.broadcast_to(n_ref[hd, ii:ii + 1, lanes], (PEER_GROUP, PEER_LANE_CHUNK)).astype(BF16)
                    d_b = jnp.broadcast_to(d_ref[hd, ii:ii + 1, lanes], (PEER_GROUP, PEER_LANE_CHUNK)).astype(BF16)
                    for g in range(nkeys // PEER_GROUP):
                        rows = pl.ds(g * PEER_GROUP, PEER_GROUP)
                        contrib = jnp.where(r_ref[hd, rows, lanes] < n_b, e2_ref[hd, rows, lanes] * d_b,
                                            jnp.zeros((), BF16))
                        w[g] = contrib if w[g] is None else w[g] + contrib
                for g in range(nkeys // PEER_GROUP):
                    r0 = bi * nkeys + g * PEER_GROUP
                    a_blk = act[r0:r0 + PEER_GROUP, lc * PEER_LANE_CHUNK:(lc + 1) * PEER_LANE_CHUNK]
                    p_scr[pl.ds(ii * nkeys + g * PEER_GROUP, PEER_GROUP), lanes] = w[g] * a_blk

    acc_ref[...] += _dot(vt_ref[...], p_scr[...])

    @pl.when(j == pl.num_programs(1) - 1)
    def _():
        o_ref[...] = acc_ref[...].T


def _peer(layer, xnt, u, vt, r_tab, e2_tab, n_tab, d_tab):
    d, t = xnt.shape
    n_exp = u.shape[1]
    nk = r_tab.shape[1]
    tt = min(PEER_TOKENS, t)
    te = PEER_EXPERTS_PER_STEP
    tab_spec = pl.BlockSpec((PEER_HEADS, nk, tt), lambda i, j: (0, 0, i))
    row_spec = pl.BlockSpec((PEER_HEADS, te // nk, tt), lambda i, j: (0, j, i))
    return pl.pallas_call(
        functools.partial(_peer_kernel, nkeys=nk),
        out_shape=jax.ShapeDtypeStruct((t, d), F32),
        grid=(t // tt, n_exp // te),
        in_specs=[
            pl.BlockSpec((d, tt), lambda i, j: (0, i)),
            pl.BlockSpec((None, te, d), lambda i, j: (layer, j, 0)),
            pl.BlockSpec((None, d, te), lambda i, j: (layer, 0, j)),
            tab_spec, tab_spec, row_spec, row_spec,
        ],
        out_specs=pl.BlockSpec((tt, d), lambda i, j: (i, 0)),
        scratch_shapes=[
            pltpu.VMEM((te, tt), BF16),
            pltpu.VMEM((d, tt), F32),
        ],
        compiler_params=_params(("arbitrary", "arbitrary")),
        name="peer",
    )(xnt, u, vt, r_tab, e2_tab, n_tab, d_tab)


def _ple_kernel(h_ref, y_ref, p_ref, g_ref, wg_ref, wp_ref, fg_ref, o_ref, *, final_norm):
    h = h_ref[...] + y_ref[...]
    gate = jax.nn.sigmoid(_dot(_rmsnorm(h, g_ref[...]).astype(BF16), wg_ref[...]))
    out = h + gate * _dot(p_ref[...].astype(BF16), wp_ref[...])
    if final_norm:
        out = _rmsnorm(out, fg_ref[...])
    o_ref[...] = out


def _ple(layer, batch, h, y, p, g, wg, wp, final_g, final_norm):
    t, d = h.shape
    tm = min(PLE_TOKENS, t)
    final_spec = pl.BlockSpec(final_g.shape, lambda i: (0, 0), pipeline_mode=pl.Buffered(1))
    return pl.pallas_call(
        functools.partial(_ple_kernel, final_norm=final_norm),
        out_shape=jax.ShapeDtypeStruct((t, d), F32),
        grid=(t // tm,),
        in_specs=[pl.BlockSpec((tm, d), lambda i: (i, 0)), pl.BlockSpec((tm, d), lambda i: (i, 0)),
                  pl.BlockSpec((None, None, tm, p.shape[-1]), lambda i: (layer, batch, i, 0)),
                  _layer_spec(g, layer), _layer_spec(wg, layer), _layer_spec(wp, layer), final_spec],
        out_specs=pl.BlockSpec((tm, d), lambda i: (i, 0)),
        compiler_params=_params(("arbitrary",)),
        name="ple",
    )(h, y, p, g, wg, wp, final_g)


def _block_diag(w, group):
    layers, heads, n, _ = w.shape
    w = w.reshape(layers, heads // group, group, n, n)
    eye = jnp.eye(group, dtype=w.dtype)
    return jnp.einsum("lbgij,gk->lbgikj", w, eye).reshape(layers, heads // group, group * n, group * n)


def kernel(x, p, norm_mix_g, w_in, conv_w, conv_b, lru_wa, lru_ba, lru_wx, lru_bx, lru_lam, pool_w, pool_scale,
           w_out, norm_ffn_g, peer_wq, peer_keys, peer_u, peer_v, norm_ple_g, ple_wp, ple_wg, final_g):
    batch, seq, d = x.shape
    depth = w_in.shape[0]
    group = MXU_DIM // lru_wa.shape[-1]
    rows = lambda v: v.reshape(depth, 1, -1)
    bf = lambda v: v.astype(BF16)
    mix_args = (rows(norm_mix_g), bf(w_in), conv_w, rows(conv_b), bf(_block_diag(lru_wa, group)), rows(lru_ba),
                bf(_block_diag(lru_wx, group)), rows(lru_bx), rows(lru_lam), bf(pool_w), rows(pool_scale), bf(w_out))
    route_args = (rows(norm_ffn_g), bf(peer_wq), bf(peer_keys))
    u_all = bf(peer_u)
    vt_all = bf(jnp.swapaxes(peer_v, 1, 2))
    ple_args = (rows(norm_ple_g), bf(ple_wg), bf(ple_wp), final_g.reshape(1, -1))
    outs = []
    for b in range(batch):
        h = x[b]
        for i in range(depth):
            h = _mix(i, h, *mix_args)
            xnt, r_tab, e2_tab, n_tab, d_tab = _route(i, h, *route_args)
            y = _peer(i, xnt, u_all, vt_all, r_tab, e2_tab, n_tab, d_tab)
            h = _ple(i, b, h, y, p, *ple_args, final_norm=(i == depth - 1))
        outs.append(h)
    return jnp.stack(outs, axis=0)
```

```python
import functools
import math

import jax
import jax.numpy as jnp
from jax import lax
from jax.experimental import pallas as pl
from jax.experimental.pallas import tpu as pltpu

F32 = jnp.float32
BF16 = jnp.bfloat16

CONV_WIDTH = 4
LRU_HEADS = 16
LRU_C = 8.0
POOL_WINDOWS = (2, 4, 8, 16)
PEER_HEADS = 8
PEER_NKEYS = 128
PEER_TOPK = 16
RMS_EPS = 1e-6

SUBLANES = 8
LANES = 128
MXU_DIM = 256
VMEM_LIMIT_BYTES = 56 * 1024 * 1024

MIX_TOKENS = 256
ROUTE_TOKENS = 256
PEER_TOKENS = 512
PEER_EXPERTS_PER_STEP = 1024
PLE_TOKENS = 512

CONV_HALO = SUBLANES
POOL_HALO = 2 * SUBLANES


def _rmsnorm(x, g):
    return x * lax.rsqrt(jnp.mean(x * x, axis=-1, keepdims=True) + RMS_EPS) * g


def _dot(a, b):
    return jnp.dot(a, b, preferred_element_type=F32)


def _dot_nt(a, b):
    return lax.dot_general(a, b, (((1,), (1,)), ((), ())), preferred_element_type=F32)


def _layer_spec(arr, layer):
    index = (layer,) + (0,) * (arr.ndim - 1)
    return pl.BlockSpec((None,) + arr.shape[1:], lambda *_: index, pipeline_mode=pl.Buffered(1))


def _params(semantics):
    return pltpu.CompilerParams(dimension_semantics=semantics, vmem_limit_bytes=VMEM_LIMIT_BYTES)


def _mix_kernel(h_ref, g_ref, win_ref, convw_ref, convb_ref, wa_ref, ba_ref, wx_ref, bx_ref, lam_ref,
                poolw_ref, pscale_ref, wout_ref, o_ref,
                xbuf, pbuf, hstate, cat_ref, *, d_lru, d_pool):
    tm = h_ref.shape[0]
    step = pl.program_id(0)

    @pl.when(step == 0)
    def _():
        xbuf[0:CONV_HALO, :] = jnp.zeros((CONV_HALO, d_lru), F32)
        pbuf[0:POOL_HALO, :] = jnp.zeros((POOL_HALO, d_pool), F32)
        hstate[...] = jnp.zeros_like(hstate)

    h = h_ref[...]
    hn = _rmsnorm(h, g_ref[...]).astype(BF16)

    pbuf[POOL_HALO:POOL_HALO + tm, :] = _dot(hn, win_ref[:, 2 * d_lru:])
    xbuf[CONV_HALO:CONV_HALO + tm, :] = _dot(hn, win_ref[:, 0:d_lru])

    n_groups = len(POOL_WINDOWS)
    gdim = d_pool // n_groups
    frames = step * tm + 1 + lax.broadcasted_iota(jnp.int32, (tm, gdim), 0)
    for gi, win in enumerate(POOL_WINDOWS):
        cols = slice(gi * gdim, (gi + 1) * gdim)
        cur = pbuf[POOL_HALO:POOL_HALO + tm, cols]
        s = cur
        for k in range(1, win):
            s = s + pbuf[POOL_HALO - k:POOL_HALO - k + tm, cols]
        count = jnp.minimum(frames, win).astype(F32)
        d = (s / count - cur).astype(BF16)
        y = _dot(d, poolw_ref[gi]) * pscale_ref[:, cols]
        cat_ref[:, d_lru + gi * gdim:d_lru + (gi + 1) * gdim] = y.astype(BF16)
    pbuf[0:POOL_HALO, :] = pbuf[tm:tm + POOL_HALO, :]
    o_ref[...] = h + _dot(cat_ref[:, d_lru:], wout_ref[d_lru:, :])

    x = convb_ref[...] + xbuf[CONV_HALO:CONV_HALO + tm, :] * convw_ref[CONV_WIDTH - 1:CONV_WIDTH, :]
    for k in range(1, CONV_WIDTH):
        x = x + xbuf[CONV_HALO - k:CONV_HALO - k + tm, :] * convw_ref[CONV_WIDTH - 1 - k:CONV_WIDTH - k, :]
    xbuf[0:CONV_HALO, :] = xbuf[tm:tm + CONV_HALO, :]

    xb = x.astype(BF16)
    n_blk = d_lru // MXU_DIM
    r_parts, i_parts = [], []
    for b in range(n_blk):
        xs = xb[:, b * MXU_DIM:(b + 1) * MXU_DIM]
        r_parts.append(_dot(xs, wa_ref[b]))
        i_parts.append(_dot(xs, wx_ref[b]))
    y_gate = _dot(hn, win_ref[:, d_lru:2 * d_lru])
    r = jax.nn.sigmoid(jnp.concatenate(r_parts, axis=1) + ba_ref[...])
    ig = jax.nn.sigmoid(jnp.concatenate(i_parts, axis=1) + bx_ref[...])
    lam = lam_ref[...]
    softplus_neg_lam = jnp.maximum(-lam, 0.0) + jnp.log(1.0 + jnp.exp(-jnp.abs(lam)))
    log_a = (-LRU_C) * r * softplus_neg_lam
    a = jnp.exp(log_a)
    mult = jnp.sqrt(1.0 - jnp.exp(2.0 * log_a))
    b_in = mult * ig * x

    row = lax.broadcasted_iota(jnp.int32, (tm, d_lru), 0)
    shift = 1
    while shift < tm:
        if shift < SUBLANES:
            keep = row >= shift
            a_prev = jnp.where(keep, pltpu.roll(a, shift, axis=0), 1.0)
            b_prev = jnp.where(keep, pltpu.roll(b_in, shift, axis=0), 0.0)
        else:
            a_prev = jnp.concatenate([jnp.ones((shift, d_lru), F32), a[:tm - shift]], axis=0)
            b_prev = jnp.concatenate([jnp.zeros((shift, d_lru), F32), b_in[:tm - shift]], axis=0)
        b_in = a * b_prev + b_in
        a = a * a_prev
        shift *= 2
    hs = a * hstate[...] + b_in
    hstate[...] = hs[tm - 1:tm, :]
    cat_ref[:, 0:d_lru] = (hs * jax.nn.gelu(y_gate)).astype(BF16)
    o_ref[...] += _dot(cat_ref[:, 0:d_lru], wout_ref[0:d_lru, :])


def _mix(layer, h, g, w_in, conv_w, conv_b, wa_bd, ba, wx_bd, bx, lam, pool_w, pool_scale, w_out):
    t, d = h.shape
    d_lru = conv_w.shape[-1]
    d_pool = pool_scale.shape[-1]
    tm = min(MIX_TOKENS, t)
    row_spec = pl.BlockSpec((tm, d), lambda i: (i, 0))
    args = (g, w_in, conv_w, conv_b, wa_bd, ba, wx_bd, bx, lam, pool_w, pool_scale, w_out)
    return pl.pallas_call(
        functools.partial(_mix_kernel, d_lru=d_lru, d_pool=d_pool),
        out_shape=jax.ShapeDtypeStruct((t, d), F32),
        grid=(t // tm,),
        in_specs=[row_spec] + [_layer_spec(a, layer) for a in args],
        out_specs=row_spec,
        scratch_shapes=[
            pltpu.VMEM((tm + CONV_HALO, d_lru), F32),
            pltpu.VMEM((tm + POOL_HALO, d_pool), F32),
            pltpu.VMEM((1, d_lru), F32),
            pltpu.VMEM((tm, d_lru + d_pool), BF16),
        ],
        compiler_params=_params(("arbitrary",)),
        name="mix",
    )(h, *args)


def _sort16_network():
    pairs = []

    def merge(lo, n, r):
        step = r * 2
        if step < n:
            merge(lo, n, step)
            merge(lo + r, n, step)
            for i in range(lo + r, lo + n - r, step):
                pairs.append((i, i + r))
        else:
            pairs.append((lo, lo + r))

    def sort(lo, n):
        if n > 1:
            m = n // 2
            sort(lo, m)
            sort(lo + m, m)
            merge(lo, n, 1)

    sort(0, PEER_TOPK)
    return tuple(pairs)


_SORT16 = _sort16_network()


def _sort_desc(vals):
    vals = list(vals)
    for i, j in _SORT16:
        hi = jnp.maximum(vals[i], vals[j])
        lo = jnp.minimum(vals[i], vals[j])
        vals[i], vals[j] = hi, lo
    return vals


def _bitonic_desc(vals):
    vals = list(vals)
    n = len(vals)
    gap = n // 2
    while gap >= 1:
        for i in range(n):
            if (i // gap) % 2 == 0:
                hi = jnp.maximum(vals[i], vals[i + gap])
                lo = jnp.minimum(vals[i], vals[i + gap])
                vals[i], vals[i + gap] = hi, lo
        gap //= 2
    return vals


def _top16_rows(s_ref, idx):
    groups = [s_ref[idx, pl.ds(SUBLANES * v, SUBLANES), :] for v in range(PEER_NKEYS // SUBLANES)]
    vals = _sort_desc(groups)
    shift = SUBLANES // 2
    while shift >= 1:
        other = [pltpu.roll(v, shift, axis=0) for v in vals]
        merged = [jnp.maximum(vals[k], other[PEER_TOPK - 1 - k]) for k in range(PEER_TOPK)]
        vals = _bitonic_desc(merged)
        shift //= 2
    return vals


def _kth_largest_rows(arrays):
    n = len(arrays)
    vals = list(arrays)
    for i, j in _SORT16:
        if j < n:
            hi = jnp.maximum(vals[i], vals[j])
            lo = jnp.minimum(vals[i], vals[j])
            vals[i], vals[j] = hi, lo
    vals = vals + [None] * (PEER_TOPK - n)
    shift = SUBLANES // 2
    while shift >= 1:
        other = [None if v is None else pltpu.roll(v, shift, axis=0) for v in vals]
        merged = []
        for k in range(PEER_TOPK):
            a, b = vals[k], other[PEER_TOPK - 1 - k]
            merged.append(b if a is None else a if b is None else jnp.maximum(a, b))
        if shift > 1:
            vals = _bitonic_desc(merged)
        shift //= 2
    out = merged[0]
    for v in merged[1:]:
        out = jnp.minimum(out, v)
    return out


def _pack_rows(vals):
    sub = lax.broadcasted_iota(jnp.int32, vals[0].shape, 0)
    out = vals[SUBLANES - 1]
    for k in range(SUBLANES - 2, -1, -1):
        out = jnp.where(sub == k, vals[k], out)
    return out


def _route_kernel(h_ref, g_ref, wq_ref, keys_ref, xnt_ref, r_ref, e2_ref, n_ref, d_ref, s_scr):
    tr = h_ref.shape[0]
    xn_f32 = _rmsnorm(h_ref[...], g_ref[...])
    xnt_ref[...] = xn_f32.T.astype(BF16)
    xn = xn_f32.astype(BF16)
    q = _dot(xn, wq_ref[...]).astype(BF16)
    half = keys_ref.shape[2]
    for hd in range(PEER_HEADS):
        for c in range(2):
            col = (hd * 2 + c) * half
            s_scr[hd * 2 + c] = _dot_nt(keys_ref[c], q[:, col:col + half])

    n_groups = PEER_NKEYS // SUBLANES

    def per_head(hd, carry):
        sv1 = _top16_rows(s_scr, 2 * hd)
        sv2 = _top16_rows(s_scr, 2 * hd + 1)
        p1_hi = _pack_rows(sv1[SUBLANES:])
        p2_lo = _pack_rows(sv2[:SUBLANES])
        p2_hi = _pack_rows(sv2[SUBLANES:])
        cands = [sv1[0] + p2_lo, sv1[0] + p2_hi, p1_hi + sv2[0]]
        cands += [sv1[a] + p2_lo for a in range(1, SUBLANES)]
        m1, m2 = sv1[0], sv2[0]
        tau = _kth_largest_rows(cands)
        zterms = None
        for c in cands:
            term = jnp.where(c >= tau, jnp.exp(c - (m1 + m2)), 0.0)
            zterms = term if zterms is None else zterms + term
        inv_z = 1.0 / jnp.sum(zterms, axis=0, keepdims=True)

        half_k = PEER_TOPK // 2
        count_top = jnp.full((SUBLANES, tr), float(PEER_TOPK), F32)
        for k in range(PEER_TOPK - 1, -1, -1):
            count_top = jnp.where(m1 + sv2[k] < tau, float(k), count_top)
        for v in range(n_groups):
            rows = pl.ds(SUBLANES * v, SUBLANES)
            s1g = s_scr[2 * hd, rows, :]
            s2g = s_scr[2 * hd + 1, rows, :]
            rank = jnp.full((SUBLANES, tr), float(PEER_TOPK), F32)
            count = jnp.full((SUBLANES, tr), float(half_k), F32)
            for k in range(PEER_TOPK - 1, -1, -1):
                rank = jnp.where(s2g >= sv2[k], float(k), rank)
                if k < half_k:
                    count = jnp.where(s1g + sv2[k] < tau, float(k), count)
            count = jnp.where(s1g >= m1, count_top, count)
            n_ref[hd, rows, :] = count
            d_ref[hd, rows, :] = jnp.exp(s1g - m1) * inv_z
            s_scr[2 * hd, rows, :] = rank
            s_scr[2 * hd + 1, rows, :] = jnp.exp(s2g - m2)
        r_ref[hd] = s_scr[2 * hd].astype(BF16)
        e2_ref[hd] = s_scr[2 * hd + 1].astype(BF16)
        return carry

    lax.fori_loop(0, PEER_HEADS, per_head, 0)


def _route(layer, h, g, wq, keys):
    t, d = h.shape
    tr = min(ROUTE_TOKENS, t)
    nk = keys.shape[2]
    tab_spec = pl.BlockSpec((PEER_HEADS, nk, tr), lambda i: (0, 0, i))
    return pl.pallas_call(
        _route_kernel,
        out_shape=(
            jax.ShapeDtypeStruct((d, t), BF16),
            jax.ShapeDtypeStruct((PEER_HEADS, nk, t), BF16),
            jax.ShapeDtypeStruct((PEER_HEADS, nk, t), BF16),
            jax.ShapeDtypeStruct((PEER_HEADS, nk, t), F32),
            jax.ShapeDtypeStruct((PEER_HEADS, nk, t), F32),
        ),
        grid=(t // tr,),
        in_specs=[pl.BlockSpec((tr, d), lambda i: (i, 0)), _layer_spec(g, layer), _layer_spec(wq, layer),
                  _layer_spec(keys, layer)],
        out_specs=(pl.BlockSpec((d, tr), lambda i: (0, i)), tab_spec, tab_spec, tab_spec, tab_spec),
        scratch_shapes=[pltpu.VMEM((2 * PEER_HEADS, nk, tr), F32)],
        compiler_params=_params(("arbitrary",)),
        name="route",
    )(h, g, wq, keys)


PEER_GROUP = 2 * SUBLANES
PEER_LANE_CHUNK = 256
PEER_PIECE = MXU_DIM


def _peer_kernel(xnt_ref, u_ref, vt_ref, r_ref, e2_ref, n_ref, d_ref, o_ref, p_scr, acc_ref, *, nkeys):
    j = pl.program_id(1)
    te, tt = p_scr.shape
    blocks_per_piece = PEER_PIECE // nkeys

    @pl.when(j == 0)
    def _():
        acc_ref[...] = jnp.zeros_like(acc_ref)

    for piece in range(te // PEER_PIECE):
        prow = pl.ds(piece * PEER_PIECE, PEER_PIECE)
        act = jax.nn.gelu(_dot(u_ref[prow, :], xnt_ref[...])).astype(BF16)
        for bi in range(blocks_per_piece):
            ii = piece * blocks_per_piece + bi
            for lc in range(tt // PEER_LANE_CHUNK):
                lanes = pl.ds(lc * PEER_LANE_CHUNK, PEER_LANE_CHUNK)
                w = [None] * (nkeys // PEER_GROUP)
                for hd in range(PEER_HEADS):
                    n_b = jnp.broadcast_to(n_ref[hd, ii:ii + 1, lanes], (PEER_GROUP, PEER_LANE_CHUNK)).astype(BF16)
                    d_b = jnp.broadcast_to(d_ref[hd, ii:ii + 1, lanes], (PEER_GROUP, PEER_LANE_CHUNK)).astype(BF16)
                    for g in range(nkeys // PEER_GROUP):
                        rows = pl.ds(g * PEER_GROUP, PEER_GROUP)
                        contrib = jnp.where(r_ref[hd, rows, lanes] < n_b, e2_ref[hd, rows, lanes] * d_b,
                                            jnp.zeros((), BF16))
                        w[g] = contrib if w[g] is None else w[g] + contrib
                for g in range(nkeys // PEER_GROUP):
                    r0 = bi * nkeys + g * PEER_GROUP
                    a_blk = act[r0:r0 + PEER_GROUP, lc * PEER_LANE_CHUNK:(lc + 1) * PEER_LANE_CHUNK]
                    p_scr[pl.ds(ii * nkeys + g * PEER_GROUP, PEER_GROUP), lanes] = w[g] * a_blk

    acc_ref[...] += _dot(vt_ref[...], p_scr[...])

    @pl.when(j == pl.num_programs(1) - 1)
    def _():
        o_ref[...] = acc_ref[...].T


def _peer(layer, xnt, u, vt, r_tab, e2_tab, n_tab, d_tab):
    d, t = xnt.shape
    n_exp = u.shape[1]
    nk = r_tab.shape[1]
    tt = min(PEER_TOKENS, t)
    te = PEER_EXPERTS_PER_STEP
    tab_spec = pl.BlockSpec((PEER_HEADS, nk, tt), lambda i, j: (0, 0, i))
    row_spec = pl.BlockSpec((PEER_HEADS, te // nk, tt), lambda i, j: (0, j, i))
    return pl.pallas_call(
        functools.partial(_peer_kernel, nkeys=nk),
        out_shape=jax.ShapeDtypeStruct((t, d), F32),
        grid=(t // tt, n_exp // te),
        in_specs=[
            pl.BlockSpec((d, tt), lambda i, j: (0, i)),
            pl.BlockSpec((None, te, d), lambda i, j: (layer, j, 0)),
            pl.BlockSpec((None, d, te), lambda i, j: (layer, 0, j)),
            tab_spec, tab_spec, row_spec, row_spec,
        ],
        out_specs=pl.BlockSpec((tt, d), lambda i, j: (i, 0)),
        scratch_shapes=[
            pltpu.VMEM((te, tt), BF16),
            pltpu.VMEM((d, tt), F32),
        ],
        compiler_params=_params(("arbitrary", "arbitrary")),
        name="peer",
    )(xnt, u, vt, r_tab, e2_tab, n_tab, d_tab)


def _ple_kernel(h_ref, y_ref, p_ref, g_ref, wg_ref, wp_ref, fg_ref, o_ref, *, final_norm):
    h = h_ref[...] + y_ref[...]
    gate = jax.nn.sigmoid(_dot(_rmsnorm(h, g_ref[...]).astype(BF16), wg_ref[...]))
    out = h + gate * _dot(p_ref[...].astype(BF16), wp_ref[...])
    if final_norm:
        out = _rmsnorm(out, fg_ref[...])
    o_ref[...] = out


def _ple(layer, batch, h, y, p, g, wg, wp, final_g, final_norm):
    t, d = h.shape
    tm = min(PLE_TOKENS, t)
    final_spec = pl.BlockSpec(final_g.shape, lambda i: (0, 0), pipeline_mode=pl.Buffered(1))
    return pl.pallas_call(
        functools.partial(_ple_kernel, final_norm=final_norm),
        out_shape=jax.ShapeDtypeStruct((t, d), F32),
        grid=(t // tm,),
        in_specs=[pl.BlockSpec((tm, d), lambda i: (i, 0)), pl.BlockSpec((tm, d), lambda i: (i, 0)),
                  pl.BlockSpec((None, None, tm, p.shape[-1]), lambda i: (layer, batch, i, 0)),
                  _layer_spec(g, layer), _layer_spec(wg, layer), _layer_spec(wp, layer), final_spec],
        out_specs=pl.BlockSpec((tm, d), lambda i: (i, 0)),
        compiler_params=_params(("arbitrary",)),
        name="ple",
    )(h, y, p, g, wg, wp, final_g)


def _block_diag(w, group):
    layers, heads, n, _ = w.shape
    w = w.reshape(layers, heads // group, group, n, n)
    eye = jnp.eye(group, dtype=w.dtype)
    return jnp.einsum("lbgij,gk->lbgikj", w, eye).reshape(layers, heads // group, group * n, group * n)


def kernel(x, p, norm_mix_g, w_in, conv_w, conv_b, lru_wa, lru_ba, lru_wx, lru_bx, lru_lam, pool_w, pool_scale,
           w_out, norm_ffn_g, peer_wq, peer_keys, peer_u, peer_v, norm_ple_g, ple_wp, ple_wg, final_g):
    batch, seq, d = x.shape
    depth = w_in.shape[0]
    group = MXU_DIM // lru_wa.shape[-1]
    rows = lambda v: v.reshape(depth, 1, -1)
    bf = lambda v: v.astype(BF16)
    mix_args = (rows(norm_mix_g), bf(w_in), conv_w, rows(conv_b), bf(_block_diag(lru_wa, group)), rows(lru_ba),
                bf(_block_diag(lru_wx, group)), rows(lru_bx), rows(lru_lam), bf(pool_w), rows(pool_scale), bf(w_out))
    route_args = (rows(norm_ffn_g), bf(peer_wq), bf(peer_keys))
    u_all = bf(peer_u)
    vt_all = bf(jnp.swapaxes(peer_v, 1, 2))
    ple_args = (rows(norm_ple_g), bf(ple_wg), bf(ple_wp), final_g.reshape(1, -1))
    outs = []
    for b in range(batch):
        h = x[b]
        for i in range(depth):
            h = _mix(i, h, *mix_args)
            xnt, r_tab, e2_tab, n_tab, d_tab = _route(i, h, *route_args)
            y = _peer(i, xnt, u_all, vt_all, r_tab, e2_tab, n_tab, d_tab)
            h = _ple(i, b, h, y, p, *ple_args, final_norm=(i == depth - 1))
        outs.append(h)
    return jnp.stack(outs, axis=0)
```

```python
import functools
import math

import jax
import jax.numpy as jnp
from jax import lax
from jax.experimental import pallas as pl
from jax.experimental.pallas import tpu as pltpu

F32 = jnp.float32
BF16 = jnp.bfloat16

CONV_WIDTH = 4
LRU_HEADS = 16
LRU_C = 8.0
POOL_WINDOWS = (2, 4, 8, 16)
PEER_HEADS = 8
PEER_NKEYS = 128
PEER_TOPK = 16
RMS_EPS = 1e-6

SUBLANES = 8
LANES = 128
MXU_DIM = 256
VMEM_LIMIT_BYTES = 56 * 1024 * 1024

MIX_TOKENS = 256
ROUTE_TOKENS = 256
PEER_TOKENS = 512
PEER_EXPERTS_PER_STEP = 2048
PLE_TOKENS = 512

CONV_HALO = SUBLANES
POOL_HALO = 2 * SUBLANES


def _rmsnorm(x, g):
    return x * lax.rsqrt(jnp.mean(x * x, axis=-1, keepdims=True) + RMS_EPS) * g


def _dot(a, b):
    return jnp.dot(a, b, preferred_element_type=F32)


def _dot_nt(a, b):
    return lax.dot_general(a, b, (((1,), (1,)), ((), ())), preferred_element_type=F32)


def _layer_spec(arr, layer):
    index = (layer,) + (0,) * (arr.ndim - 1)
    return pl.BlockSpec((None,) + arr.shape[1:], lambda *_: index, pipeline_mode=pl.Buffered(1))


def _params(semantics):
    return pltpu.CompilerParams(dimension_semantics=semantics, vmem_limit_bytes=VMEM_LIMIT_BYTES)


def _mix_kernel(h_ref, g_ref, win_ref, convw_ref, convb_ref, wa_ref, ba_ref, wx_ref, bx_ref, lam_ref,
                poolw_ref, pscale_ref, wout_ref, o_ref,
                xbuf, pbuf, hstate, cat_ref, *, d_lru, d_pool):
    tm = h_ref.shape[0]
    step = pl.program_id(0)

    @pl.when(step == 0)
    def _():
        xbuf[0:CONV_HALO, :] = jnp.zeros((CONV_HALO, d_lru), F32)
        pbuf[0:POOL_HALO, :] = jnp.zeros((POOL_HALO, d_pool), F32)
        hstate[...] = jnp.zeros_like(hstate)

    h = h_ref[...]
    hn = _rmsnorm(h, g_ref[...]).astype(BF16)

    pbuf[POOL_HALO:POOL_HALO + tm, :] = _dot(hn, win_ref[:, 2 * d_lru:])
    xbuf[CONV_HALO:CONV_HALO + tm, :] = _dot(hn, win_ref[:, 0:d_lru])

    n_groups = len(POOL_WINDOWS)
    gdim = d_pool // n_groups
    frames = step * tm + 1 + lax.broadcasted_iota(jnp.int32, (tm, gdim), 0)
    for gi, win in enumerate(POOL_WINDOWS):
        cols = slice(gi * gdim, (gi + 1) * gdim)
        cur = pbuf[POOL_HALO:POOL_HALO + tm, cols]
        s = cur
        for k in range(1, win):
            s = s + pbuf[POOL_HALO - k:POOL_HALO - k + tm, cols]
        count = jnp.minimum(frames, win).astype(F32)
        d = (s / count - cur).astype(BF16)
        y = _dot(d, poolw_ref[gi]) * pscale_ref[:, cols]
        cat_ref[:, d_lru + gi * gdim:d_lru + (gi + 1) * gdim] = y.astype(BF16)
    pbuf[0:POOL_HALO, :] = pbuf[tm:tm + POOL_HALO, :]
    o_ref[...] = h + _dot(cat_ref[:, d_lru:], wout_ref[d_lru:, :])

    x = convb_ref[...] + xbuf[CONV_HALO:CONV_HALO + tm, :] * convw_ref[CONV_WIDTH - 1:CONV_WIDTH, :]
    for k in range(1, CONV_WIDTH):
        x = x + xbuf[CONV_HALO - k:CONV_HALO - k + tm, :] * convw_ref[CONV_WIDTH - 1 - k:CONV_WIDTH - k, :]
    xbuf[0:CONV_HALO, :] = xbuf[tm:tm + CONV_HALO, :]

    xb = x.astype(BF16)
    n_blk = d_lru // MXU_DIM
    r_parts, i_parts = [], []
    for b in range(n_blk):
        xs = xb[:, b * MXU_DIM:(b + 1) * MXU_DIM]
        r_parts.append(_dot(xs, wa_ref[b]))
        i_parts.append(_dot(xs, wx_ref[b]))
    y_gate = _dot(hn, win_ref[:, d_lru:2 * d_lru])
    r = jax.nn.sigmoid(jnp.concatenate(r_parts, axis=1) + ba_ref[...])
    ig = jax.nn.sigmoid(jnp.concatenate(i_parts, axis=1) + bx_ref[...])
    lam = lam_ref[...]
    softplus_neg_lam = jnp.maximum(-lam, 0.0) + jnp.log(1.0 + jnp.exp(-jnp.abs(lam)))
    log_a = (-LRU_C) * r * softplus_neg_lam
    a = jnp.exp(log_a)
    mult = jnp.sqrt(1.0 - jnp.exp(2.0 * log_a))
    b_in = mult * ig * x

    row = lax.broadcasted_iota(jnp.int32, (tm, d_lru), 0)
    shift = 1
    while shift < tm:
        if shift < SUBLANES:
            keep = row >= shift
            a_prev = jnp.where(keep, pltpu.roll(a, shift, axis=0), 1.0)
            b_prev = jnp.where(keep, pltpu.roll(b_in, shift, axis=0), 0.0)
        else:
            a_prev = jnp.concatenate([jnp.ones((shift, d_lru), F32), a[:tm - shift]], axis=0)
            b_prev = jnp.concatenate([jnp.zeros((shift, d_lru), F32), b_in[:tm - shift]], axis=0)
        b_in = a * b_prev + b_in
        a = a * a_prev
        shift *= 2
    hs = a * hstate[...] + b_in
    hstate[...] = hs[tm - 1:tm, :]
    cat_ref[:, 0:d_lru] = (hs * jax.nn.gelu(y_gate)).astype(BF16)
    o_ref[...] += _dot(cat_ref[:, 0:d_lru], wout_ref[0:d_lru, :])


def _mix(layer, h, g, w_in, conv_w, conv_b, wa_bd, ba, wx_bd, bx, lam, pool_w, pool_scale, w_out):
    t, d = h.shape
    d_lru = conv_w.shape[-1]
    d_pool = pool_scale.shape[-1]
    tm = min(MIX_TOKENS, t)
    row_spec = pl.BlockSpec((tm, d), lambda i: (i, 0))
    args = (g, w_in, conv_w, conv_b, wa_bd, ba, wx_bd, bx, lam, pool_w, pool_scale, w_out)
    return pl.pallas_call(
        functools.partial(_mix_kernel, d_lru=d_lru, d_pool=d_pool),
        out_shape=jax.ShapeDtypeStruct((t, d), F32),
        grid=(t // tm,),
        in_specs=[row_spec] + [_layer_spec(a, layer) for a in args],
        out_specs=row_spec,
        scratch_shapes=[
            pltpu.VMEM((tm + CONV_HALO, d_lru), F32),
            pltpu.VMEM((tm + POOL_HALO, d_pool), F32),
            pltpu.VMEM((1, d_lru), F32),
            pltpu.VMEM((tm, d_lru + d_pool), BF16),
        ],
        compiler_params=_params(("arbitrary",)),
        name="mix",
    )(h, *args)


def _sort16_network():
    pairs = []

    def merge(lo, n, r):
        step = r * 2
        if step < n:
            merge(lo, n, step)
            merge(lo + r, n, step)
            for i in range(lo + r, lo + n - r, step):
                pairs.append((i, i + r))
        else:
            pairs.append((lo, lo + r))

    def sort(lo, n):
        if n > 1:
            m = n // 2
            sort(lo, m)
            sort(lo + m, m)
            merge(lo, n, 1)

    sort(0, PEER_TOPK)
    return tuple(pairs)


_SORT16 = _sort16_network()


def _sort_desc(vals):
    vals = list(vals)
    for i, j in _SORT16:
        hi = jnp.maximum(vals[i], vals[j])
        lo = jnp.minimum(vals[i], vals[j])
        vals[i], vals[j] = hi, lo
    return vals


def _bitonic_desc(vals):
    vals = list(vals)
    n = len(vals)
    gap = n // 2
    while gap >= 1:
        for i in range(n):
            if (i // gap) % 2 == 0:
                hi = jnp.maximum(vals[i], vals[i + gap])
                lo = jnp.minimum(vals[i], vals[i + gap])
                vals[i], vals[i + gap] = hi, lo
        gap //= 2
    return vals


def _top16_rows(s_ref, idx):
    groups = [s_ref[idx, pl.ds(SUBLANES * v, SUBLANES), :] for v in range(PEER_NKEYS // SUBLANES)]
    vals = _sort_desc(groups)
    shift = SUBLANES // 2
    while shift >= 1:
        other = [pltpu.roll(v, shift, axis=0) for v in vals]
        merged = [jnp.maximum(vals[k], other[PEER_TOPK - 1 - k]) for k in range(PEER_TOPK)]
        vals = _bitonic_desc(merged)
        shift //= 2
    return vals


def _kth_largest_rows(arrays):
    n = len(arrays)
    vals = list(arrays)
    for i, j in _SORT16:
        if j < n:
            hi = jnp.maximum(vals[i], vals[j])
            lo = jnp.minimum(vals[i], vals[j])
            vals[i], vals[j] = hi, lo
    vals = vals + [None] * (PEER_TOPK - n)
    shift = SUBLANES // 2
    while shift >= 1:
        other = [None if v is None else pltpu.roll(v, shift, axis=0) for v in vals]
        merged = []
        for k in range(PEER_TOPK):
            a, b = vals[k], other[PEER_TOPK - 1 - k]
            merged.append(b if a is None else a if b is None else jnp.maximum(a, b))
        if shift > 1:
            vals = _bitonic_desc(merged)
        shift //= 2
    out = merged[0]
    for v in merged[1:]:
        out = jnp.minimum(out, v)
    return out


def _pack_rows(vals):
    sub = lax.broadcasted_iota(jnp.int32, vals[0].shape, 0)
    out = vals[SUBLANES - 1]
    for k in range(SUBLANES - 2, -1, -1):
        out = jnp.where(sub == k, vals[k], out)
    return out


def _route_kernel(h_ref, g_ref, wq_ref, keys_ref, xnt_ref, r_ref, e2_ref, n_ref, d_ref, s_scr):
    tr = h_ref.shape[0]
    xn_f32 = _rmsnorm(h_ref[...], g_ref[...])
    xnt_ref[...] = xn_f32.T.astype(BF16)
    xn = xn_f32.astype(BF16)
    q = _dot(xn, wq_ref[...]).astype(BF16)
    half = keys_ref.shape[2]
    for hd in range(PEER_HEADS):
        for c in range(2):
            col = (hd * 2 + c) * half
            s_scr[hd * 2 + c] = _dot_nt(keys_ref[c], q[:, col:col + half])

    n_groups = PEER_NKEYS // SUBLANES

    def per_head(hd, carry):
        sv1 = _top16_rows(s_scr, 2 * hd)
        sv2 = _top16_rows(s_scr, 2 * hd + 1)
        p1_hi = _pack_rows(sv1[SUBLANES:])
        p2_lo = _pack_rows(sv2[:SUBLANES])
        p2_hi = _pack_rows(sv2[SUBLANES:])
        cands = [sv1[0] + p2_lo, sv1[0] + p2_hi, p1_hi + sv2[0]]
        cands += [sv1[a] + p2_lo for a in range(1, SUBLANES)]
        m1, m2 = sv1[0], sv2[0]
        tau = _kth_largest_rows(cands)
        zterms = None
        for c in cands:
            term = jnp.where(c >= tau, jnp.exp(c - (m1 + m2)), 0.0)
            zterms = term if zterms is None else zterms + term
        inv_z = 1.0 / jnp.sum(zterms, axis=0, keepdims=True)

        half_k = PEER_TOPK // 2
        count_top = jnp.full((SUBLANES, tr), float(PEER_TOPK), F32)
        for k in range(PEER_TOPK - 1, -1, -1):
            count_top = jnp.where(m1 + sv2[k] < tau, float(k), count_top)
        for v in range(n_groups):
            rows = pl.ds(SUBLANES * v, SUBLANES)
            s1g = s_scr[2 * hd, rows, :]
            s2g = s_scr[2 * hd + 1, rows, :]
            rank = jnp.full((SUBLANES, tr), float(PEER_TOPK), F32)
            count = jnp.full((SUBLANES, tr), float(half_k), F32)
            for k in range(PEER_TOPK - 1, -1, -1):
                rank = jnp.where(s2g >= sv2[k], float(k), rank)
                if k < half_k:
                    count = jnp.where(s1g + sv2[k] < tau, float(k), count)
            count = jnp.where(s1g >= m1, count_top, count)
            n_ref[hd, rows, :] = count
            d_ref[hd, rows, :] = jnp.exp(s1g - m1) * inv_z
            s_scr[2 * hd, rows, :] = rank
            s_scr[2 * hd + 1, rows, :] = jnp.exp(s2g - m2)
        r_ref[hd] = s_scr[2 * hd].astype(BF16)
        e2_ref[hd] = s_scr[2 * hd + 1].astype(BF16)
        return carry

    lax.fori_loop(0, PEER_HEADS, per_head, 0)


def _route(layer, h, g, wq, keys):
    t, d = h.shape
    tr = min(ROUTE_TOKENS, t)
    nk = keys.shape[2]
    tab_spec = pl.BlockSpec((PEER_HEADS, nk, tr), lambda i: (0, 0, i))
    return pl.pallas_call(
        _route_kernel,
        out_shape=(
            jax.ShapeDtypeStruct((d, t), BF16),
            jax.ShapeDtypeStruct((PEER_HEADS, nk, t), BF16),
            jax.ShapeDtypeStruct((PEER_HEADS, nk, t), BF16),
            jax.ShapeDtypeStruct((PEER_HEADS, nk, t), F32),
            jax.ShapeDtypeStruct((PEER_HEADS, nk, t), F32),
        ),
        grid=(t // tr,),
        in_specs=[pl.BlockSpec((tr, d), lambda i: (i, 0)), _layer_spec(g, layer), _layer_spec(wq, layer),
                  _layer_spec(keys, layer)],
        out_specs=(pl.BlockSpec((d, tr), lambda i: (0, i)), tab_spec, tab_spec, tab_spec, tab_spec),
        scratch_shapes=[pltpu.VMEM((2 * PEER_HEADS, nk, tr), F32)],
        compiler_params=_params(("arbitrary",)),
        name="route",
    )(h, g, wq, keys)


PEER_GROUP = 2 * SUBLANES
PEER_LANE_CHUNK = 256
PEER_PIECE = MXU_DIM


def _peer_kernel(xnt_ref, u_ref, vt_ref, r_ref, e2_ref, n_ref, d_ref, o_ref, p_scr, acc_ref, *, nkeys):
    j = pl.program_id(1)
    te, tt = p_scr.shape
    blocks_per_piece = PEER_PIECE // nkeys

    @pl.when(j == 0)
    def _():
        acc_ref[...] = jnp.zeros_like(acc_ref)

    for piece in range(te // PEER_PIECE):
        prow = pl.ds(piece * PEER_PIECE, PEER_PIECE)
        act = jax.nn.gelu(_dot(u_ref[prow, :], xnt_ref[...])).astype(BF16)
        for bi in range(blocks_per_piece):
            ii = piece * blocks_per_piece + bi
            for lc in range(tt // PEER_LANE_CHUNK):
                lanes = pl.ds(lc * PEER_LANE_CHUNK, PEER_LANE_CHUNK)
                w = [None] * (nkeys // PEER_GROUP)
                for hd in range(PEER_HEADS):
                    n_b = jnp.broadcast_to(n_ref[hd, ii:ii + 1, lanes], (PEER_GROUP, PEER_LANE_CHUNK)).astype(BF16)
                    d_b = jnp.broadcast_to(d_ref[hd, ii:ii + 1, lanes], (PEER_GROUP, PEER_LANE_CHUNK)).astype(BF16)
                    for g in range(nkeys // PEER_GROUP):
                        rows = pl.ds(g * PEER_GROUP, PEER_GROUP)
                        contrib = jnp.where(r_ref[hd, rows, lanes] < n_b, e2_ref[hd, rows, lanes] * d_b,
                                            jnp.zeros((), BF16))
                        w[g] = contrib if w[g] is None else w[g] + contrib
                for g in range(nkeys // PEER_GROUP):
                    r0 = bi * nkeys + g * PEER_GROUP
                    a_blk = act[r0:r0 + PEER_GROUP, lc * PEER_LANE_CHUNK:(lc + 1) * PEER_LANE_CHUNK]
                    p_scr[pl.ds(ii * nkeys + g * PEER_GROUP, PEER_GROUP), lanes] = w[g] * a_blk

    acc_ref[...] += _dot(vt_ref[...], p_scr[...])

    @pl.when(j == pl.num_programs(1) - 1)
    def _():
        o_ref[...] = acc_ref[...].T.astype(o_ref.dtype)


def _peer(layer, xnt, u, vt, r_tab, e2_tab, n_tab, d_tab):
    d, t = xnt.shape
    n_exp = u.shape[1]
    nk = r_tab.shape[1]
    tt = min(PEER_TOKENS, t)
    te = PEER_EXPERTS_PER_STEP
    tab_spec = pl.BlockSpec((PEER_HEADS, nk, tt), lambda i, j: (0, 0, i), pipeline_mode=pl.Buffered(1))
    row_spec = pl.BlockSpec((PEER_HEADS, te // nk, tt), lambda i, j: (0, j, i))
    return pl.pallas_call(
        functools.partial(_peer_kernel, nkeys=nk),
        out_shape=jax.ShapeDtypeStruct((t, d), BF16),
        grid=(t // tt, n_exp // te),
        in_specs=[
            pl.BlockSpec((d, tt), lambda i, j: (0, i)),
            pl.BlockSpec((None, te, d), lambda i, j: (layer, j, 0)),
            pl.BlockSpec((None, d, te), lambda i, j: (layer, 0, j)),
            tab_spec, tab_spec, row_spec, row_spec,
        ],
        out_specs=pl.BlockSpec((tt, d), lambda i, j: (i, 0)),
        scratch_shapes=[
            pltpu.VMEM((te, tt), BF16),
            pltpu.VMEM((d, tt), F32),
        ],
        compiler_params=_params(("arbitrary", "arbitrary")),
        name="peer",
    )(xnt, u, vt, r_tab, e2_tab, n_tab, d_tab)


def _ple_kernel(h_ref, y_ref, p_ref, g_ref, wg_ref, wp_ref, fg_ref, o_ref, *, final_norm):
    h = h_ref[...] + y_ref[...]
    gate = jax.nn.sigmoid(_dot(_rmsnorm(h, g_ref[...]).astype(BF16), wg_ref[...]))
    out = h + gate * _dot(p_ref[...].astype(BF16), wp_ref[...])
    if final_norm:
        out = _rmsnorm(out, fg_ref[...])
    o_ref[...] = out


def _ple(layer, batch, h, y, p, g, wg, wp, final_g, final_norm):
    t, d = h.shape
    tm = min(PLE_TOKENS, t)
    final_spec = pl.BlockSpec(final_g.shape, lambda i: (0, 0), pipeline_mode=pl.Buffered(1))
    return pl.pallas_call(
        functools.partial(_ple_kernel, final_norm=final_norm),
        out_shape=jax.ShapeDtypeStruct((t, d), F32),
        grid=(t // tm,),
        in_specs=[pl.BlockSpec((tm, d), lambda i: (i, 0)), pl.BlockSpec((tm, d), lambda i: (i, 0)),
                  pl.BlockSpec((None, None, tm, p.shape[-1]), lambda i: (layer, batch, i, 0)),
                  _layer_spec(g, layer), _layer_spec(wg, layer), _layer_spec(wp, layer), final_spec],
        out_specs=pl.BlockSpec((tm, d), lambda i: (i, 0)),
        compiler_params=_params(("arbitrary",)),
        name="ple",
    )(h, y, p, g, wg, wp, final_g)


def _block_diag(w, group):
    layers, heads, n, _ = w.shape
    w = w.reshape(layers, heads // group, group, n, n)
    eye = jnp.eye(group, dtype=w.dtype)
    return jnp.einsum("lbgij,gk->lbgikj", w, eye).reshape(layers, heads // group, group * n, group * n)


def kernel(x, p, norm_mix_g, w_in, conv_w, conv_b, lru_wa, lru_ba, lru_wx, lru_bx, lru_lam, pool_w, pool_scale,
           w_out, norm_ffn_g, peer_wq, peer_keys, peer_u, peer_v, norm_ple_g, ple_wp, ple_wg, final_g):
    batch, seq, d = x.shape
    depth = w_in.shape[0]
    group = MXU_DIM // lru_wa.shape[-1]
    rows = lambda v: v.reshape(depth, 1, -1)
    bf = lambda v: v.astype(BF16)
    mix_args = (rows(norm_mix_g), bf(w_in), conv_w, rows(conv_b), bf(_block_diag(lru_wa, group)), rows(lru_ba),
                bf(_block_diag(lru_wx, group)), rows(lru_bx), rows(lru_lam), bf(pool_w), rows(pool_scale), bf(w_out))
    route_args = (rows(norm_ffn_g), bf(peer_wq), bf(peer_keys))
    u_all = bf(peer_u)
    vt_all = bf(jnp.swapaxes(peer_v, 1, 2))
    ple_args = (rows(norm_ple_g), bf(ple_wg), bf(ple_wp), final_g.reshape(1, -1))
    outs = []
    for b in range(batch):
        h = x[b]
        for i in range(depth):
            h = _mix(i, h, *mix_args)
            xnt, r_tab, e2_tab, n_tab, d_tab = _route(i, h, *route_args)
            y = _peer(i, xnt, u_all, vt_all, r_tab, e2_tab, n_tab, d_tab)
            h = _ple(i, b, h, y, p, *ple_args, final_norm=(i == depth - 1))
        outs.append(h)
    return jnp.stack(outs, axis=0)
```

```python
import functools
import math

import jax
import jax.numpy as jnp
from jax import lax
from jax.experimental import pallas as pl
from jax.experimental.pallas import tpu as pltpu

F32 = jnp.float32
BF16 = jnp.bfloat16

CONV_WIDTH = 4
LRU_HEADS = 16
LRU_C = 8.0
POOL_WINDOWS = (2, 4, 8, 16)
PEER_HEADS = 8
PEER_NKEYS = 128
PEER_TOPK = 16
RMS_EPS = 1e-6

SUBLANES = 8
LANES = 128
MXU_DIM = 256
VMEM_LIMIT_BYTES = 56 * 1024 * 1024

MIX_TOKENS = 256
ROUTE_TOKENS = 256
PEER_TOKENS = 512
PEER_EXPERTS_PER_STEP = 2048
PLE_TOKENS = 512

CONV_HALO = SUBLANES
POOL_HALO = 2 * SUBLANES
assert all(w & (w - 1) == 0 and w <= POOL_HALO for w in POOL_WINDOWS)


def _rmsnorm(x, g):
    return x * lax.rsqrt(jnp.mean(x * x, axis=-1, keepdims=True) + RMS_EPS) * g


def _dot(a, b):
    return jnp.dot(a, b, preferred_element_type=F32)


def _dot_nt(a, b):
    return lax.dot_general(a, b, (((1,), (1,)), ((), ())), preferred_element_type=F32)


def _layer_spec(arr, layer):
    index = (layer,) + (0,) * (arr.ndim - 1)
    return pl.BlockSpec((None,) + arr.shape[1:], lambda *_: index, pipeline_mode=pl.Buffered(1))


def _params(semantics):
    return pltpu.CompilerParams(dimension_semantics=semantics, vmem_limit_bytes=VMEM_LIMIT_BYTES)


def _mix_kernel(h_ref, g_ref, win_ref, convw_ref, convb_ref, wa_ref, ba_ref, wx_ref, bx_ref, lam_ref,
                poolw_ref, pscale_ref, wout_ref, o_ref,
                xbuf, pbuf, hstate, cat_ref, *, d_lru, d_pool):
    tm = h_ref.shape[0]
    step = pl.program_id(0)

    @pl.when(step == 0)
    def _():
        xbuf[0:CONV_HALO, :] = jnp.zeros((CONV_HALO, d_lru), F32)
        pbuf[0:POOL_HALO, :] = jnp.zeros((POOL_HALO, d_pool), F32)
        hstate[...] = jnp.zeros_like(hstate)

    h = h_ref[...]
    hn = _rmsnorm(h, g_ref[...]).astype(BF16)

    pbuf[POOL_HALO:POOL_HALO + tm, :] = _dot(hn, win_ref[:, 2 * d_lru:])
    xbuf[CONV_HALO:CONV_HALO + tm, :] = _dot(hn, win_ref[:, 0:d_lru])

    n_groups = len(POOL_WINDOWS)
    gdim = d_pool // n_groups
    frames = step * tm + 1 + lax.broadcasted_iota(jnp.int32, (tm, gdim), 0)
    for gi, win in enumerate(POOL_WINDOWS):
        cols = slice(gi * gdim, (gi + 1) * gdim)
        ext = pbuf[:, cols]
        s = ext
        k = 1
        while k < win:
            s = s + pltpu.roll(s, k, axis=0)
            k *= 2
        s = s[POOL_HALO:, :]
        cur = ext[POOL_HALO:, :]
        count = jnp.minimum(frames, win).astype(F32)
        d = (s / count - cur).astype(BF16)
        y = _dot(d, poolw_ref[gi]) * pscale_ref[:, cols]
        cat_ref[:, d_lru + gi * gdim:d_lru + (gi + 1) * gdim] = y.astype(BF16)
    pbuf[0:POOL_HALO, :] = pbuf[tm:tm + POOL_HALO, :]

    x = convb_ref[...] + xbuf[CONV_HALO:CONV_HALO + tm, :] * convw_ref[CONV_WIDTH - 1:CONV_WIDTH, :]
    for k in range(1, CONV_WIDTH):
        x = x + xbuf[CONV_HALO - k:CONV_HALO - k + tm, :] * convw_ref[CONV_WIDTH - 1 - k:CONV_WIDTH - k, :]
    xbuf[0:CONV_HALO, :] = xbuf[tm:tm + CONV_HALO, :]

    xb = x.astype(BF16)
    n_blk = d_lru // MXU_DIM
    r_parts, i_parts = [], []
    for b in range(n_blk):
        xs = xb[:, b * MXU_DIM:(b + 1) * MXU_DIM]
        r_parts.append(_dot(xs, wa_ref[b]))
        i_parts.append(_dot(xs, wx_ref[b]))
    r = jax.nn.sigmoid(jnp.concatenate(r_parts, axis=1) + ba_ref[...])
    ig = jax.nn.sigmoid(jnp.concatenate(i_parts, axis=1) + bx_ref[...])
    lam = lam_ref[...]
    softplus_neg_lam = jnp.maximum(-lam, 0.0) + jnp.log(1.0 + jnp.exp(-jnp.abs(lam)))
    log_a = (-LRU_C) * r * softplus_neg_lam
    a = jnp.exp(log_a)
    mult = jnp.sqrt(1.0 - jnp.exp(2.0 * log_a))
    b_in = mult * ig * x

    y_gate = _dot(hn, win_ref[:, d_lru:2 * d_lru])
    o_ref[...] = h + _dot(cat_ref[:, d_lru:], wout_ref[d_lru:, :])

    row = lax.broadcasted_iota(jnp.int32, (tm, d_lru), 0)
    shift = 1
    while shift < tm:
        if shift < SUBLANES:
            keep = row >= shift
            a_prev = jnp.where(keep, pltpu.roll(a, shift, axis=0), 1.0)
            b_prev = jnp.where(keep, pltpu.roll(b_in, shift, axis=0), 0.0)
        else:
            a_prev = jnp.concatenate([jnp.ones((shift, d_lru), F32), a[:tm - shift]], axis=0)
            b_prev = jnp.concatenate([jnp.zeros((shift, d_lru), F32), b_in[:tm - shift]], axis=0)
        b_in = a * b_prev + b_in
        a = a * a_prev
        shift *= 2
    hs = a * hstate[...] + b_in
    hstate[...] = hs[tm - 1:tm, :]
    cat_ref[:, 0:d_lru] = (hs * jax.nn.gelu(y_gate)).astype(BF16)
    o_ref[...] += _dot(cat_ref[:, 0:d_lru], wout_ref[0:d_lru, :])


def _mix(layer, h, g, w_in, conv_w, conv_b, wa_bd, ba, wx_bd, bx, lam, pool_w, pool_scale, w_out):
    t, d = h.shape
    d_lru = conv_w.shape[-1]
    d_pool = pool_scale.shape[-1]
    tm = min(MIX_TOKENS, t)
    row_spec = pl.BlockSpec((tm, d), lambda i: (i, 0))
    args = (g, w_in, conv_w, conv_b, wa_bd, ba, wx_bd, bx, lam, pool_w, pool_scale, w_out)
    return pl.pallas_call(
        functools.partial(_mix_kernel, d_lru=d_lru, d_pool=d_pool),
        out_shape=jax.ShapeDtypeStruct((t, d), F32),
        grid=(t // tm,),
        in_specs=[row_spec] + [_layer_spec(a, layer) for a in args],
        out_specs=row_spec,
        scratch_shapes=[
            pltpu.VMEM((tm + CONV_HALO, d_lru), F32),
            pltpu.VMEM((tm + POOL_HALO, d_pool), F32),
            pltpu.VMEM((1, d_lru), F32),
            pltpu.VMEM((tm, d_lru + d_pool), BF16),
        ],
        compiler_params=_params(("arbitrary",)),
        name="mix",
    )(h, *args)


def _sort16_network():
    pairs = []

    def merge(lo, n, r):
        step = r * 2
        if step < n:
            merge(lo, n, step)
            merge(lo + r, n, step)
            for i in range(lo + r, lo + n - r, step):
                pairs.append((i, i + r))
        else:
            pairs.append((lo, lo + r))

    def sort(lo, n):
        if n > 1:
            m = n // 2
            sort(lo, m)
            sort(lo + m, m)
            merge(lo, n, 1)

    sort(0, PEER_TOPK)
    return tuple(pairs)


_SORT16 = _sort16_network()


def _sort_desc(vals):
    vals = list(vals)
    for i, j in _SORT16:
        hi = jnp.maximum(vals[i], vals[j])
        lo = jnp.minimum(vals[i], vals[j])
        vals[i], vals[j] = hi, lo
    return vals


def _bitonic_desc(vals):
    vals = list(vals)
    n = len(vals)
    gap = n // 2
    while gap >= 1:
        for i in range(n):
            if (i // gap) % 2 == 0:
                hi = jnp.maximum(vals[i], vals[i + gap])
                lo = jnp.minimum(vals[i], vals[i + gap])
                vals[i], vals[i + gap] = hi, lo
        gap //= 2
    return vals


def _top16_rows(s_ref, idx):
    groups = [s_ref[idx, pl.ds(SUBLANES * v, SUBLANES), :] for v in range(PEER_NKEYS // SUBLANES)]
    vals = _sort_desc(groups)
    shift = SUBLANES // 2
    while shift >= 1:
        other = [pltpu.roll(v, shift, axis=0) for v in vals]
        merged = [jnp.maximum(vals[k], other[PEER_TOPK - 1 - k]) for k in range(PEER_TOPK)]
        vals = _bitonic_desc(merged)
        shift //= 2
    return vals


def _kth_largest_rows(arrays):
    n = len(arrays)
    vals = list(arrays)
    for i, j in _SORT16:
        if j < n:
            hi = jnp.maximum(vals[i], vals[j])
            lo = jnp.minimum(vals[i], vals[j])
            vals[i], vals[j] = hi, lo
    vals = vals + [None] * (PEER_TOPK - n)
    shift = SUBLANES // 2
    while shift >= 1:
        other = [None if v is None else pltpu.roll(v, shift, axis=0) for v in vals]
        merged = []
        for k in range(PEER_TOPK):
            a, b = vals[k], other[PEER_TOPK - 1 - k]
            merged.append(b if a is None else a if b is None else jnp.maximum(a, b))
        if shift > 1:
            vals = _bitonic_desc(merged)
        shift //= 2
    out = merged[0]
    for v in merged[1:]:
        out = jnp.minimum(out, v)
    return out


def _pack_rows(vals):
    sub = lax.broadcasted_iota(jnp.int32, vals[0].shape, 0)
    out = vals[SUBLANES - 1]
    for k in range(SUBLANES - 2, -1, -1):
        out = jnp.where(sub == k, vals[k], out)
    return out


def _route_kernel(h_ref, g_ref, wq_ref, keys_ref, xnt_ref, r_ref, e2_ref, n_ref, d_ref, s_scr):
    tr = h_ref.shape[0]
    xn_f32 = _rmsnorm(h_ref[...], g_ref[...])
    xnt_ref[...] = xn_f32.T.astype(BF16)
    xn = xn_f32.astype(BF16)
    q = _dot(xn, wq_ref[...]).astype(BF16)
    half = keys_ref.shape[2]
    for hd in range(PEER_HEADS):
        for c in range(2):
            col = (hd * 2 + c) * half
            s_scr[hd * 2 + c] = _dot_nt(keys_ref[c], q[:, col:col + half])

    n_groups = PEER_NKEYS // SUBLANES

    def per_head(hd, carry):
        sv1 = _top16_rows(s_scr, 2 * hd)
        sv2 = _top16_rows(s_scr, 2 * hd + 1)
        p1_hi = _pack_rows(sv1[SUBLANES:])
        p2_lo = _pack_rows(sv2[:SUBLANES])
        p2_hi = _pack_rows(sv2[SUBLANES:])
        cands = [sv1[0] + p2_lo, sv1[0] + p2_hi, p1_hi + sv2[0]]
        cands += [sv1[a] + p2_lo for a in range(1, SUBLANES)]
        m1, m2 = sv1[0], sv2[0]
        tau = _kth_largest_rows(cands)
        zterms = None
        for c in cands:
            term = jnp.where(c >= tau, jnp.exp(c - (m1 + m2)), 0.0)
            zterms = term if zterms is None else zterms + term
        inv_z = 1.0 / jnp.sum(zterms, axis=0, keepdims=True)

        half_k = PEER_TOPK // 2
        count_top = jnp.full((SUBLANES, tr), float(PEER_TOPK), F32)
        for k in range(PEER_TOPK - 1, -1, -1):
            count_top = jnp.where(m1 + sv2[k] < tau, float(k), count_top)
        for v in range(n_groups):
            rows = pl.ds(SUBLANES * v, SUBLANES)
            s1g = s_scr[2 * hd, rows, :]
            s2g = s_scr[2 * hd + 1, rows, :]
            rank = jnp.full((SUBLANES, tr), float(PEER_TOPK), F32)
            count = jnp.full((SUBLANES, tr), float(half_k), F32)
            for k in range(PEER_TOPK - 1, -1, -1):
                rank = jnp.where(s2g >= sv2[k], float(k), rank)
                if k < half_k:
                    count = jnp.where(s1g + sv2[k] < tau, float(k), count)
            count = jnp.where(s1g >= m1, count_top, count)
            n_ref[hd, rows, :] = count
            d_ref[hd, rows, :] = jnp.exp(s1g - m1) * inv_z
            s_scr[2 * hd, rows, :] = rank
            s_scr[2 * hd + 1, rows, :] = jnp.exp(s2g - m2)
        r_ref[hd] = s_scr[2 * hd].astype(BF16)
        e2_ref[hd] = s_scr[2 * hd + 1].astype(BF16)
        return carry

    lax.fori_loop(0, PEER_HEADS, per_head, 0)


def _route(layer, h, g, wq, keys):
    t, d = h.shape
    tr = min(ROUTE_TOKENS, t)
    nk = keys.shape[2]
    tab_spec = pl.BlockSpec((PEER_HEADS, nk, tr), lambda i: (0, 0, i))
    return pl.pallas_call(
        _route_kernel,
        out_shape=(
            jax.ShapeDtypeStruct((d, t), BF16),
            jax.ShapeDtypeStruct((PEER_HEADS, nk, t), BF16),
            jax.ShapeDtypeStruct((PEER_HEADS, nk, t), BF16),
            jax.ShapeDtypeStruct((PEER_HEADS, nk, t), F32),
            jax.ShapeDtypeStruct((PEER_HEADS, nk, t), F32),
        ),
        grid=(t // tr,),
        in_specs=[pl.BlockSpec((tr, d), lambda i: (i, 0)), _layer_spec(g, layer), _layer_spec(wq, layer),
                  _layer_spec(keys, layer)],
        out_specs=(pl.BlockSpec((d, tr), lambda i: (0, i)), tab_spec, tab_spec, tab_spec, tab_spec),
        scratch_shapes=[pltpu.VMEM((2 * PEER_HEADS, nk, tr), F32)],
        compiler_params=_params(("arbitrary",)),
        name="route",
    )(h, g, wq, keys)


PEER_GROUP = 2 * SUBLANES
PEER_LANE_CHUNK = 256
PEER_PIECE = MXU_DIM


def _peer_kernel(xnt_ref, u_ref, vt_ref, r_ref, e2_ref, n_ref, d_ref, o_ref, p_scr, acc_ref, *, nkeys):
    j = pl.program_id(1)
    te, tt = p_scr.shape
    blocks_per_piece = PEER_PIECE // nkeys

    @pl.when(j == 0)
    def _():
        acc_ref[...] = jnp.zeros_like(acc_ref)

    for piece in range(te // PEER_PIECE):
        prow = pl.ds(piece * PEER_PIECE, PEER_PIECE)
        act = jax.nn.gelu(_dot(u_ref[prow, :], xnt_ref[...])).astype(BF16)
        for bi in range(blocks_per_piece):
            ii = piece * blocks_per_piece + bi
            for lc in range(tt // PEER_LANE_CHUNK):
                lanes = pl.ds(lc * PEER_LANE_CHUNK, PEER_LANE_CHUNK)
                w = [None] * (nkeys // PEER_GROUP)
                for hd in range(PEER_HEADS):
                    n_b = jnp.broadcast_to(n_ref[hd, ii:ii + 1, lanes], (PEER_GROUP, PEER_LANE_CHUNK)).astype(BF16)
                    d_b = jnp.broadcast_to(d_ref[hd, ii:ii + 1, lanes], (PEER_GROUP, PEER_LANE_CHUNK)).astype(BF16)
                    for g in range(nkeys // PEER_GROUP):
                        rows = pl.ds(g * PEER_GROUP, PEER_GROUP)
                        contrib = jnp.where(r_ref[hd, rows, lanes] < n_b, e2_ref[hd, rows, lanes] * d_b,
                                            jnp.zeros((), BF16))
                        w[g] = contrib if w[g] is None else w[g] + contrib
                for g in range(nkeys // PEER_GROUP):
                    r0 = bi * nkeys + g * PEER_GROUP
                    a_blk = act[r0:r0 + PEER_GROUP, lc * PEER_LANE_CHUNK:(lc + 1) * PEER_LANE_CHUNK]
                    p_scr[pl.ds(ii * nkeys + g * PEER_GROUP, PEER_GROUP), lanes] = w[g] * a_blk

    half_te = te // 2
    acc_ref[...] += _dot(vt_ref[:, 0:half_te], p_scr[0:half_te, :])
    acc_ref[...] += _dot(vt_ref[:, half_te:], p_scr[half_te:, :])

    @pl.when(j == pl.num_programs(1) - 1)
    def _():
        o_ref[...] = acc_ref[...].T.astype(o_ref.dtype)


def _peer(layer, xnt, u, vt, r_tab, e2_tab, n_tab, d_tab):
    d, t = xnt.shape
    n_exp = u.shape[1]
    nk = r_tab.shape[1]
    tt = min(PEER_TOKENS, t)
    te = PEER_EXPERTS_PER_STEP
    tab_spec = pl.BlockSpec((PEER_HEADS, nk, tt), lambda i, j: (0, 0, i), pipeline_mode=pl.Buffered(1))
    row_spec = pl.BlockSpec((PEER_HEADS, te // nk, tt), lambda i, j: (0, j, i))
    return pl.pallas_call(
        functools.partial(_peer_kernel, nkeys=nk),
        out_shape=jax.ShapeDtypeStruct((t, d), BF16),
        grid=(t // tt, n_exp // te),
        in_specs=[
            pl.BlockSpec((d, tt), lambda i, j: (0, i)),
            pl.BlockSpec((None, te, d), lambda i, j: (layer, j, 0)),
            pl.BlockSpec((None, d, te), lambda i, j: (layer, 0, j)),
            tab_spec, tab_spec, row_spec, row_spec,
        ],
        out_specs=pl.BlockSpec((tt, d), lambda i, j: (i, 0)),
        scratch_shapes=[
            pltpu.VMEM((te, tt), BF16),
            pltpu.VMEM((d, tt), F32),
        ],
        compiler_params=_params(("arbitrary", "arbitrary")),
        name="peer",
    )(xnt, u, vt, r_tab, e2_tab, n_tab, d_tab)


def _ple_kernel(h_ref, y_ref, p_ref, g_ref, wg_ref, wp_ref, fg_ref, o_ref, *, final_norm):
    h = h_ref[...] + y_ref[...]
    gate = jax.nn.sigmoid(_dot(_rmsnorm(h, g_ref[...]).astype(BF16), wg_ref[...]))
    out = h + gate * _dot(p_ref[...].astype(BF16), wp_ref[...])
    if final_norm:
        out = _rmsnorm(out, fg_ref[...])
    o_ref[...] = out


def _ple(layer, batch, h, y, p, g, wg, wp, final_g, final_norm):
    t, d = h.shape
    tm = min(PLE_TOKENS, t)
    final_spec = pl.BlockSpec(final_g.shape, lambda i: (0, 0), pipeline_mode=pl.Buffered(1))
    return pl.pallas_call(
        functools.partial(_ple_kernel, final_norm=final_norm),
        out_shape=jax.ShapeDtypeStruct((t, d), F32),
        grid=(t // tm,),
        in_specs=[pl.BlockSpec((tm, d), lambda i: (i, 0)), pl.BlockSpec((tm, d), lambda i: (i, 0)),
                  pl.BlockSpec((None, None, tm, p.shape[-1]), lambda i: (layer, batch, i, 0)),
                  _layer_spec(g, layer), _layer_spec(wg, layer), _layer_spec(wp, layer), final_spec],
        out_specs=pl.BlockSpec((tm, d), lambda i: (i, 0)),
        compiler_params=_params(("arbitrary",)),
        name="ple",
    )(h, y, p, g, wg, wp, final_g)


def _block_diag(w, group):
    layers, heads, n, _ = w.shape
    w = w.reshape(layers, heads // group, group, n, n)
    eye = jnp.eye(group, dtype=w.dtype)
    return jnp.einsum("lbgij,gk->lbgikj", w, eye).reshape(layers, heads // group, group * n, group * n)


def kernel(x, p, norm_mix_g, w_in, conv_w, conv_b, lru_wa, lru_ba, lru_wx, lru_bx, lru_lam, pool_w, pool_scale,
           w_out, norm_ffn_g, peer_wq, peer_keys, peer_u, peer_v, norm_ple_g, ple_wp, ple_wg, final_g):
    batch, seq, d = x.shape
    depth = w_in.shape[0]
    group = MXU_DIM // lru_wa.shape[-1]
    rows = lambda v: v.reshape(depth, 1, -1)
    bf = lambda v: v.astype(BF16)
    mix_args = (rows(norm_mix_g), bf(w_in), conv_w, rows(conv_b), bf(_block_diag(lru_wa, group)), rows(lru_ba),
                bf(_block_diag(lru_wx, group)), rows(lru_bx), rows(lru_lam), bf(pool_w), rows(pool_scale), bf(w_out))
    route_args = (rows(norm_ffn_g), bf(peer_wq), bf(peer_keys))
    u_all = bf(peer_u)
    vt_all = bf(jnp.swapaxes(peer_v, 1, 2))
    ple_args = (rows(norm_ple_g), bf(ple_wg), bf(ple_wp), final_g.reshape(1, -1))
    outs = []
    for b in range(batch):
        h = x[b]
        for i in range(depth):
            h = _mix(i, h, *mix_args)
            xnt, r_tab, e2_tab, n_tab, d_tab = _route(i, h, *route_args)
            y = _peer(i, xnt, u_all, vt_all, r_tab, e2_tab, n_tab, d_tab)
            h = _ple(i, b, h, y, p, *ple_args, final_norm=(i == depth - 1))
        outs.append(h)
    return jnp.stack(outs, axis=0)
```

```python
import functools

import jax
import jax.numpy as jnp
from jax import lax
from jax.experimental import pallas as pl
from jax.experimental.pallas import tpu as pltpu

F32 = jnp.float32
BF16 = jnp.bfloat16

CONV_WIDTH = 4
LRU_C = 8.0
POOL_WINDOWS = (2, 4, 8, 16)
PEER_HEADS = 8
PEER_NKEYS = 128
PEER_TOPK = 16
RMS_EPS = 1e-6

SUBLANES = 8
MXU_DIM = 256
V7X_VMEM_BYTES = 64 * 1024 * 1024
VMEM_LIMIT_BYTES = V7X_VMEM_BYTES * 7 // 8

MIX_TOKENS = 256
ROUTE_TOKENS = 256
PEER_TOKENS = 512
PEER_EXPERTS_PER_STEP = 2048
PLE_TOKENS = 512

CONV_HALO = SUBLANES
POOL_HALO = 2 * SUBLANES
assert all(w & (w - 1) == 0 and w <= POOL_HALO for w in POOL_WINDOWS)


def _rmsnorm(x, g):
    return x * lax.rsqrt(jnp.mean(x * x, axis=-1, keepdims=True) + RMS_EPS) * g


def _dot(a, b):
    return jnp.dot(a, b, preferred_element_type=F32)


def _dot_nt(a, b):
    return lax.dot_general(a, b, (((1,), (1,)), ((), ())), preferred_element_type=F32)


def _layer_spec(arr, layer):
    index = (layer,) + (0,) * (arr.ndim - 1)
    return pl.BlockSpec((None,) + arr.shape[1:], lambda *_: index, pipeline_mode=pl.Buffered(1))


def _params(semantics):
    return pltpu.CompilerParams(dimension_semantics=semantics, vmem_limit_bytes=VMEM_LIMIT_BYTES)


def _mix_kernel(h_ref, g_ref, win_ref, convw_ref, convb_ref, wa_ref, ba_ref, wx_ref, bx_ref, lam_ref,
                poolw_ref, pscale_ref, wout_ref, o_ref,
                xbuf, pbuf, hstate, cat_ref, *, d_lru, d_pool):
    tm = h_ref.shape[0]
    step = pl.program_id(0)

    @pl.when(step == 0)
    def _():
        xbuf[0:CONV_HALO, :] = jnp.zeros((CONV_HALO, d_lru), F32)
        pbuf[0:POOL_HALO, :] = jnp.zeros((POOL_HALO, d_pool), F32)
        hstate[...] = jnp.zeros_like(hstate)

    h = h_ref[...]
    hn = _rmsnorm(h, g_ref[...]).astype(BF16)

    pbuf[POOL_HALO:POOL_HALO + tm, :] = _dot(hn, win_ref[:, 2 * d_lru:])
    xbuf[CONV_HALO:CONV_HALO + tm, :] = _dot(hn, win_ref[:, 0:d_lru])

    n_groups = len(POOL_WINDOWS)
    gdim = d_pool // n_groups
    frames = step * tm + 1 + lax.broadcasted_iota(jnp.int32, (tm, gdim), 0)
    for gi, win in enumerate(POOL_WINDOWS):
        cols = slice(gi * gdim, (gi + 1) * gdim)
        ext = pbuf[:, cols]
        s = ext
        k = 1
        while k < win:
            s = s + pltpu.roll(s, k, axis=0)
            k *= 2
        s = s[POOL_HALO:, :]
        cur = ext[POOL_HALO:, :]
        count = jnp.minimum(frames, win).astype(F32)
        d = (s / count - cur).astype(BF16)
        y = _dot(d, poolw_ref[gi]) * pscale_ref[:, cols]
        cat_ref[:, d_lru + gi * gdim:d_lru + (gi + 1) * gdim] = y.astype(BF16)
    pbuf[0:POOL_HALO, :] = pbuf[tm:tm + POOL_HALO, :]

    x = convb_ref[...] + xbuf[CONV_HALO:CONV_HALO + tm, :] * convw_ref[CONV_WIDTH - 1:CONV_WIDTH, :]
    for k in range(1, CONV_WIDTH):
        x = x + xbuf[CONV_HALO - k:CONV_HALO - k + tm, :] * convw_ref[CONV_WIDTH - 1 - k:CONV_WIDTH - k, :]
    xbuf[0:CONV_HALO, :] = xbuf[tm:tm + CONV_HALO, :]

    xb = x.astype(BF16)
    n_blk = d_lru // MXU_DIM
    r_parts, i_parts = [], []
    for b in range(n_blk):
        xs = xb[:, b * MXU_DIM:(b + 1) * MXU_DIM]
        r_parts.append(_dot(xs, wa_ref[b]))
        i_parts.append(_dot(xs, wx_ref[b]))
    r = jax.nn.sigmoid(jnp.concatenate(r_parts, axis=1) + ba_ref[...])
    ig = jax.nn.sigmoid(jnp.concatenate(i_parts, axis=1) + bx_ref[...])
    lam = lam_ref[...]
    softplus_neg_lam = jnp.maximum(-lam, 0.0) + jnp.log(1.0 + jnp.exp(-jnp.abs(lam)))
    log_a = (-LRU_C) * r * softplus_neg_lam
    a = jnp.exp(log_a)
    mult = jnp.sqrt(1.0 - jnp.exp(2.0 * log_a))
    b_in = mult * ig * x

    y_gate = _dot(hn, win_ref[:, d_lru:2 * d_lru])
    o_ref[...] = h + _dot(cat_ref[:, d_lru:], wout_ref[d_lru:, :])

    row = lax.broadcasted_iota(jnp.int32, (tm, d_lru), 0)
    shift = 1
    while shift < tm:
        if shift < SUBLANES:
            keep = row >= shift
            a_prev = jnp.where(keep, pltpu.roll(a, shift, axis=0), 1.0)
            b_prev = jnp.where(keep, pltpu.roll(b_in, shift, axis=0), 0.0)
        else:
            a_prev = jnp.concatenate([jnp.ones((shift, d_lru), F32), a[:tm - shift]], axis=0)
            b_prev = jnp.concatenate([jnp.zeros((shift, d_lru), F32), b_in[:tm - shift]], axis=0)
        b_in = a * b_prev + b_in
        a = a * a_prev
        shift *= 2
    hs = a * hstate[...] + b_in
    hstate[...] = hs[tm - 1:tm, :]
    cat_ref[:, 0:d_lru] = (hs * jax.nn.gelu(y_gate)).astype(BF16)
    o_ref[...] += _dot(cat_ref[:, 0:d_lru], wout_ref[0:d_lru, :])


def _mix(layer, h, g, w_in, conv_w, conv_b, wa_bd, ba, wx_bd, bx, lam, pool_w, pool_scale, w_out):
    t, d = h.shape
    d_lru = conv_w.shape[-1]
    d_pool = pool_scale.shape[-1]
    tm = min(MIX_TOKENS, t)
    row_spec = pl.BlockSpec((tm, d), lambda i: (i, 0))
    args = (g, w_in, conv_w, conv_b, wa_bd, ba, wx_bd, bx, lam, pool_w, pool_scale, w_out)
    return pl.pallas_call(
        functools.partial(_mix_kernel, d_lru=d_lru, d_pool=d_pool),
        out_shape=jax.ShapeDtypeStruct((t, d), F32),
        grid=(t // tm,),
        in_specs=[row_spec] + [_layer_spec(a, layer) for a in args],
        out_specs=row_spec,
        scratch_shapes=[
            pltpu.VMEM((tm + CONV_HALO, d_lru), F32),
            pltpu.VMEM((tm + POOL_HALO, d_pool), F32),
            pltpu.VMEM((1, d_lru), F32),
            pltpu.VMEM((tm, d_lru + d_pool), BF16),
        ],
        compiler_params=_params(("arbitrary",)),
        name="mix",
    )(h, *args)


def _sort16_network():
    pairs = []

    def merge(lo, n, r):
        step = r * 2
        if step < n:
            merge(lo, n, step)
            merge(lo + r, n, step)
            for i in range(lo + r, lo + n - r, step):
                pairs.append((i, i + r))
        else:
            pairs.append((lo, lo + r))

    def sort(lo, n):
        if n > 1:
            m = n // 2
            sort(lo, m)
            sort(lo + m, m)
            merge(lo, n, 1)

    sort(0, PEER_TOPK)
    return tuple(pairs)


_SORT16 = _sort16_network()


def _sort_desc(vals):
    vals = list(vals)
    for i, j in _SORT16:
        hi = jnp.maximum(vals[i], vals[j])
        lo = jnp.minimum(vals[i], vals[j])
        vals[i], vals[j] = hi, lo
    return vals


def _bitonic_desc(vals):
    vals = list(vals)
    n = len(vals)
    gap = n // 2
    while gap >= 1:
        for i in range(n):
            if (i // gap) % 2 == 0:
                hi = jnp.maximum(vals[i], vals[i + gap])
                lo = jnp.minimum(vals[i], vals[i + gap])
                vals[i], vals[i + gap] = hi, lo
        gap //= 2
    return vals


def _top16_rows(s_ref, idx):
    groups = [s_ref[idx, pl.ds(SUBLANES * v, SUBLANES), :] for v in range(PEER_NKEYS // SUBLANES)]
    vals = _sort_desc(groups)
    shift = SUBLANES // 2
    while shift >= 1:
        other = [pltpu.roll(v, shift, axis=0) for v in vals]
        merged = [jnp.maximum(vals[k], other[PEER_TOPK - 1 - k]) for k in range(PEER_TOPK)]
        vals = _bitonic_desc(merged)
        shift //= 2
    return vals


def _kth_largest_rows(arrays):
    n = len(arrays)
    vals = list(arrays)
    for i, j in _SORT16:
        if j < n:
            hi = jnp.maximum(vals[i], vals[j])
            lo = jnp.minimum(vals[i], vals[j])
            vals[i], vals[j] = hi, lo
    vals = vals + [None] * (PEER_TOPK - n)
    shift = SUBLANES // 2
    while shift >= 1:
        other = [None if v is None else pltpu.roll(v, shift, axis=0) for v in vals]
        merged = []
        for k in range(PEER_TOPK):
            a, b = vals[k], other[PEER_TOPK - 1 - k]
            merged.append(b if a is None else a if b is None else jnp.maximum(a, b))
        if shift > 1:
            vals = _bitonic_desc(merged)
        shift //= 2
    out = merged[0]
    for v in merged[1:]:
        out = jnp.minimum(out, v)
    return out


def _rank_desc(x, thr):
    sel = jnp.where
    c1 = thr[7] > x
    c2 = sel(c1, thr[11], thr[3]) > x
    c3 = sel(c1, sel(c2, thr[13], thr[9]), sel(c2, thr[5], thr[1])) > x
    t4 = sel(c1, sel(c2, sel(c3, thr[14], thr[12]), sel(c3, thr[10], thr[8])),
             sel(c2, sel(c3, thr[6], thr[4]), sel(c3, thr[2], thr[0])))
    c4 = t4 > x
    rank = sel(c1, 8.0, 0.0) + sel(c2, 4.0, 0.0) + sel(c3, 2.0, 0.0) + sel(c4, 1.0, 0.0)
    return sel(thr[15] > x, float(PEER_TOPK), rank)


def _pack_rows(vals):
    sub = lax.broadcasted_iota(jnp.int32, vals[0].shape, 0)
    out = vals[SUBLANES - 1]
    for k in range(SUBLANES - 2, -1, -1):
        out = jnp.where(sub == k, vals[k], out)
    return out


def _route_kernel(h_ref, g_ref, wq_ref, keys_ref, xnt_ref, r_ref, e2_ref, n_ref, d_ref, s_scr):
    tr = h_ref.shape[0]
    xn_f32 = _rmsnorm(h_ref[...], g_ref[...])
    xnt_ref[...] = xn_f32.T.astype(BF16)
    xn = xn_f32.astype(BF16)
    q = _dot(xn, wq_ref[...]).astype(BF16)
    half = keys_ref.shape[2]
    for hd in range(PEER_HEADS):
        for c in range(2):
            col = (hd * 2 + c) * half
            s_scr[hd * 2 + c] = _dot_nt(keys_ref[c], q[:, col:col + half])

    n_groups = PEER_NKEYS // SUBLANES

    def per_head(hd, carry):
        sv1 = _top16_rows(s_scr, 2 * hd)
        sv2 = _top16_rows(s_scr, 2 * hd + 1)
        p1_hi = _pack_rows(sv1[SUBLANES:])
        p2_lo = _pack_rows(sv2[:SUBLANES])
        p2_hi = _pack_rows(sv2[SUBLANES:])
        cands = [sv1[0] + p2_lo, sv1[0] + p2_hi, p1_hi + sv2[0]]
        cands += [sv1[a] + p2_lo for a in range(1, SUBLANES)]
        m1, m2 = sv1[0], sv2[0]
        tau = _kth_largest_rows(cands)
        zterms = None
        for c in cands:
            term = jnp.where(c >= tau, jnp.exp(c - (m1 + m2)), 0.0)
            zterms = term if zterms is None else zterms + term
        inv_z = 1.0 / jnp.sum(zterms, axis=0, keepdims=True)

        half_k = PEER_TOPK // 2
        count_top = jnp.full((SUBLANES, tr), float(PEER_TOPK), F32)
        for k in range(PEER_TOPK - 1, -1, -1):
            count_top = jnp.where(m1 + sv2[k] < tau, float(k), count_top)
        for v in range(n_groups):
            rows = pl.ds(SUBLANES * v, SUBLANES)
            s1g = s_scr[2 * hd, rows, :]
            s2g = s_scr[2 * hd + 1, rows, :]
            rank = _rank_desc(s2g, sv2)
            count = jnp.full((SUBLANES, tr), float(half_k), F32)
            for k in range(half_k - 1, -1, -1):
                count = jnp.where(s1g + sv2[k] < tau, float(k), count)
            count = jnp.where(s1g >= m1, count_top, count)
            n_ref[hd, rows, :] = count
            d_ref[hd, rows, :] = jnp.exp(s1g - m1) * inv_z
            s_scr[2 * hd, rows, :] = rank
            s_scr[2 * hd + 1, rows, :] = jnp.exp(s2g - m2)
        r_ref[hd] = s_scr[2 * hd].astype(BF16)
        e2_ref[hd] = s_scr[2 * hd + 1].astype(BF16)
        return carry

    lax.fori_loop(0, PEER_HEADS, per_head, 0)


def _route(layer, h, g, wq, keys):
    t, d = h.shape
    tr = min(ROUTE_TOKENS, t)
    nk = keys.shape[2]
    tab_spec = pl.BlockSpec((PEER_HEADS, nk, tr), lambda i: (0, 0, i))
    return pl.pallas_call(
        _route_kernel,
        out_shape=(
            jax.ShapeDtypeStruct((d, t), BF16),
            jax.ShapeDtypeStruct((PEER_HEADS, nk, t), BF16),
            jax.ShapeDtypeStruct((PEER_HEADS, nk, t), BF16),
            jax.ShapeDtypeStruct((PEER_HEADS, nk, t), F32),
            jax.ShapeDtypeStruct((PEER_HEADS, nk, t), F32),
        ),
        grid=(t // tr,),
        in_specs=[pl.BlockSpec((tr, d), lambda i: (i, 0)), _layer_spec(g, layer), _layer_spec(wq, layer),
                  _layer_spec(keys, layer)],
        out_specs=(pl.BlockSpec((d, tr), lambda i: (0, i)), tab_spec, tab_spec, tab_spec, tab_spec),
        scratch_shapes=[pltpu.VMEM((2 * PEER_HEADS, nk, tr), F32)],
        compiler_params=_params(("arbitrary",)),
        name="route",
    )(h, g, wq, keys)


PEER_GROUP = 2 * SUBLANES
PEER_LANE_CHUNK = MXU_DIM
PEER_PIECE = MXU_DIM


def _peer_kernel(xnt_ref, u_ref, vt_ref, r_ref, e2_ref, n_ref, d_ref, o_ref, p_scr, acc_ref, *, nkeys):
    j = pl.program_id(1)
    te, tt = p_scr.shape
    blocks_per_piece = PEER_PIECE // nkeys

    @pl.when(j == 0)
    def _():
        acc_ref[...] = jnp.zeros_like(acc_ref)

    for piece in range(te // PEER_PIECE):
        prow = pl.ds(piece * PEER_PIECE, PEER_PIECE)
        act = jax.nn.gelu(_dot(u_ref[prow, :], xnt_ref[...])).astype(BF16)
        for bi in range(blocks_per_piece):
            ii = piece * blocks_per_piece + bi
            for lc in range(tt // PEER_LANE_CHUNK):
                lanes = pl.ds(lc * PEER_LANE_CHUNK, PEER_LANE_CHUNK)
                w = [None] * (nkeys // PEER_GROUP)
                for hd in range(PEER_HEADS):
                    n_b = jnp.broadcast_to(n_ref[hd, ii:ii + 1, lanes], (PEER_GROUP, PEER_LANE_CHUNK)).astype(BF16)
                    d_b = jnp.broadcast_to(d_ref[hd, ii:ii + 1, lanes], (PEER_GROUP, PEER_LANE_CHUNK)).astype(BF16)
                    for g in range(nkeys // PEER_GROUP):
                        rows = pl.ds(g * PEER_GROUP, PEER_GROUP)
                        contrib = jnp.where(r_ref[hd, rows, lanes] < n_b, e2_ref[hd, rows, lanes] * d_b,
                                            jnp.zeros((), BF16))
                        w[g] = contrib if w[g] is None else w[g] + contrib
                for g in range(nkeys // PEER_GROUP):
                    r0 = bi * nkeys + g * PEER_GROUP
                    a_blk = act[r0:r0 + PEER_GROUP, lc * PEER_LANE_CHUNK:(lc + 1) * PEER_LANE_CHUNK]
                    p_scr[pl.ds(ii * nkeys + g * PEER_GROUP, PEER_GROUP), lanes] = w[g] * a_blk

    half_te = te // 2
    acc_ref[...] += _dot(vt_ref[:, 0:half_te], p_scr[0:half_te, :])
    acc_ref[...] += _dot(vt_ref[:, half_te:], p_scr[half_te:, :])

    @pl.when(j == pl.num_programs(1) - 1)
    def _():
        o_ref[...] = acc_ref[...].T.astype(o_ref.dtype)


def _peer(layer, xnt, u, vt, r_tab, e2_tab, n_tab, d_tab):
    d, t = xnt.shape
    n_exp = u.shape[1]
    nk = r_tab.shape[1]
    tt = min(PEER_TOKENS, t)
    te = PEER_EXPERTS_PER_STEP
    tab_spec = pl.BlockSpec((PEER_HEADS, nk, tt), lambda i, j: (0, 0, i), pipeline_mode=pl.Buffered(1))
    row_spec = pl.BlockSpec((PEER_HEADS, te // nk, tt), lambda i, j: (0, j, i))
    return pl.pallas_call(
        functools.partial(_peer_kernel, nkeys=nk),
        out_shape=jax.ShapeDtypeStruct((t, d), BF16),
        grid=(t // tt, n_exp // te),
        in_specs=[
            pl.BlockSpec((d, tt), lambda i, j: (0, i)),
            pl.BlockSpec((None, te, d), lambda i, j: (layer, j, 0)),
            pl.BlockSpec((None, d, te), lambda i, j: (layer, 0, j)),
            tab_spec, tab_spec, row_spec, row_spec,
        ],
        out_specs=pl.BlockSpec((tt, d), lambda i, j: (i, 0)),
        scratch_shapes=[
            pltpu.VMEM((te, tt), BF16),
            pltpu.VMEM((d, tt), F32),
        ],
        compiler_params=_params(("arbitrary", "arbitrary")),
        name="peer",
    )(xnt, u, vt, r_tab, e2_tab, n_tab, d_tab)


def _ple_kernel(h_ref, y_ref, p_ref, g_ref, wg_ref, wp_ref, fg_ref, o_ref, *, final_norm):
    h = h_ref[...] + y_ref[...]
    gate = jax.nn.sigmoid(_dot(_rmsnorm(h, g_ref[...]).astype(BF16), wg_ref[...]))
    out = h + gate * _dot(p_ref[...].astype(BF16), wp_ref[...])
    if final_norm:
        out = _rmsnorm(out, fg_ref[...])
    o_ref[...] = out


def _ple(layer, batch, h, y, p, g, wg, wp, final_g, final_norm):
    t, d = h.shape
    tm = min(PLE_TOKENS, t)
    final_spec = pl.BlockSpec(final_g.shape, lambda i: (0, 0), pipeline_mode=pl.Buffered(1))
    return pl.pallas_call(
        functools.partial(_ple_kernel, final_norm=final_norm),
        out_shape=jax.ShapeDtypeStruct((t, d), F32),
        grid=(t // tm,),
        in_specs=[pl.BlockSpec((tm, d), lambda i: (i, 0)), pl.BlockSpec((tm, d), lambda i: (i, 0)),
                  pl.BlockSpec((None, None, tm, p.shape[-1]), lambda i: (layer, batch, i, 0)),
                  _layer_spec(g, layer), _layer_spec(wg, layer), _layer_spec(wp, layer), final_spec],
        out_specs=pl.BlockSpec((tm, d), lambda i: (i, 0)),
        compiler_params=_params(("arbitrary",)),
        name="ple",
    )(h, y, p, g, wg, wp, final_g)


VT_EXPERTS_PER_STEP = 512


def _vt_kernel(v_ref, o_ref):
    o_ref[...] = v_ref[...].T.astype(BF16)


def _transpose_cast(v):
    layers, n_exp, d = v.shape
    te = VT_EXPERTS_PER_STEP
    return pl.pallas_call(
        _vt_kernel,
        out_shape=jax.ShapeDtypeStruct((layers, d, n_exp), BF16),
        grid=(layers, n_exp // te),
        in_specs=[pl.BlockSpec((None, te, d), lambda a, b: (a, b, 0))],
        out_specs=pl.BlockSpec((None, d, te), lambda a, b: (a, 0, b)),
        compiler_params=_params(("arbitrary", "arbitrary")),
        name="vt",
    )(v)


def _block_diag(w, group):
    layers, heads, n, _ = w.shape
    w = w.reshape(layers, heads // group, group, n, n)
    eye = jnp.eye(group, dtype=w.dtype)
    return jnp.einsum("lbgij,gk->lbgikj", w, eye).reshape(layers, heads // group, group * n, group * n)


def kernel(x, p, norm_mix_g, w_in, conv_w, conv_b, lru_wa, lru_ba, lru_wx, lru_bx, lru_lam, pool_w, pool_scale,
           w_out, norm_ffn_g, peer_wq, peer_keys, peer_u, peer_v, norm_ple_g, ple_wp, ple_wg, final_g):
    batch, seq, d = x.shape
    depth = w_in.shape[0]
    group = MXU_DIM // lru_wa.shape[-1]
    rows = lambda v: v.reshape(depth, 1, -1)
    bf = lambda v: v.astype(BF16)
    mix_args = (rows(norm_mix_g), bf(w_in), conv_w, rows(conv_b), bf(_block_diag(lru_wa, group)), rows(lru_ba),
                bf(_block_diag(lru_wx, group)), rows(lru_bx), rows(lru_lam), bf(pool_w), rows(pool_scale), bf(w_out))
    route_args = (rows(norm_ffn_g), bf(peer_wq), bf(peer_keys))
    u_all = bf(peer_u)
    vt_all = _transpose_cast(peer_v)
    ple_args = (rows(norm_ple_g), bf(ple_wg), bf(ple_wp), final_g.reshape(1, -1))
    outs = []
    for b in range(batch):
        h = x[b]
        for i in range(depth):
            h = _mix(i, h, *mix_args)
            xnt, r_tab, e2_tab, n_tab, d_tab = _route(i, h, *route_args)
            y = _peer(i, xnt, u_all, vt_all, r_tab, e2_tab, n_tab, d_tab)
            h = _ple(i, b, h, y, p, *ple_args, final_norm=(i == depth - 1))
        outs.append(h)
    return jnp.stack(outs, axis=0)
```

```python
import functools

import jax
import jax.numpy as jnp
from jax import lax
from jax.experimental import pallas as pl
from jax.experimental.pallas import tpu as pltpu

F32 = jnp.float32
BF16 = jnp.bfloat16

CONV_WIDTH = 4
LRU_C = 8.0
POOL_WINDOWS = (2, 4, 8, 16)
PEER_HEADS = 8
PEER_NKEYS = 128
PEER_TOPK = 16
RMS_EPS = 1e-6

SUBLANES = 8
MXU_DIM = 256
V7X_VMEM_BYTES = 64 * 1024 * 1024
VMEM_LIMIT_BYTES = V7X_VMEM_BYTES * 7 // 8

MIX_TOKENS = 256
ROUTE_TOKENS = 256
PEER_TOKENS = 512
PEER_EXPERTS_PER_STEP = 2048
PLE_TOKENS = 512

CONV_HALO = SUBLANES
POOL_HALO = 2 * SUBLANES
assert all(w & (w - 1) == 0 and w <= POOL_HALO for w in POOL_WINDOWS)


def _rmsnorm(x, g):
    return x * lax.rsqrt(jnp.mean(x * x, axis=-1, keepdims=True) + RMS_EPS) * g


def _dot(a, b):
    return jnp.dot(a, b, preferred_element_type=F32)


def _dot_nt(a, b):
    return lax.dot_general(a, b, (((1,), (1,)), ((), ())), preferred_element_type=F32)


def _layer_spec(arr, layer):
    index = (layer,) + (0,) * (arr.ndim - 1)
    return pl.BlockSpec((None,) + arr.shape[1:], lambda *_: index, pipeline_mode=pl.Buffered(1))


def _params(semantics):
    return pltpu.CompilerParams(dimension_semantics=semantics, vmem_limit_bytes=VMEM_LIMIT_BYTES)


def _mix_kernel(h_ref, g_ref, win_ref, convw_ref, convb_ref, wa_ref, ba_ref, wx_ref, bx_ref, lam_ref,
                poolw_ref, pscale_ref, wout_ref, o_ref,
                xbuf, pbuf, hstate, cat_ref, *, d_lru, d_pool):
    tm = h_ref.shape[0]
    step = pl.program_id(0)

    @pl.when(step == 0)
    def _():
        xbuf[0:CONV_HALO, :] = jnp.zeros((CONV_HALO, d_lru), F32)
        pbuf[0:POOL_HALO, :] = jnp.zeros((POOL_HALO, d_pool), F32)
        hstate[...] = jnp.zeros_like(hstate)

    h = h_ref[...]
    hn = _rmsnorm(h, g_ref[...]).astype(BF16)

    pbuf[POOL_HALO:POOL_HALO + tm, :] = _dot(hn, win_ref[:, 2 * d_lru:])
    xbuf[CONV_HALO:CONV_HALO + tm, :] = _dot(hn, win_ref[:, 0:d_lru])

    n_groups = len(POOL_WINDOWS)
    gdim = d_pool // n_groups
    frames = step * tm + 1 + lax.broadcasted_iota(jnp.int32, (tm, gdim), 0)
    for gi, win in enumerate(POOL_WINDOWS):
        cols = slice(gi * gdim, (gi + 1) * gdim)
        ext = pbuf[:, cols]
        s = ext
        k = 1
        while k < win:
            s = s + pltpu.roll(s, k, axis=0)
            k *= 2
        s = s[POOL_HALO:, :]
        cur = ext[POOL_HALO:, :]
        count = jnp.minimum(frames, win).astype(F32)
        d = (s / count - cur).astype(BF16)
        y = _dot(d, poolw_ref[gi]) * pscale_ref[:, cols]
        cat_ref[:, d_lru + gi * gdim:d_lru + (gi + 1) * gdim] = y.astype(BF16)
    pbuf[0:POOL_HALO, :] = pbuf[tm:tm + POOL_HALO, :]

    x = convb_ref[...] + xbuf[CONV_HALO:CONV_HALO + tm, :] * convw_ref[CONV_WIDTH - 1:CONV_WIDTH, :]
    for k in range(1, CONV_WIDTH):
        x = x + xbuf[CONV_HALO - k:CONV_HALO - k + tm, :] * convw_ref[CONV_WIDTH - 1 - k:CONV_WIDTH - k, :]
    xbuf[0:CONV_HALO, :] = xbuf[tm:tm + CONV_HALO, :]

    xb = x.astype(BF16)
    n_blk = d_lru // MXU_DIM
    r_parts, i_parts = [], []
    for b in range(n_blk):
        xs = xb[:, b * MXU_DIM:(b + 1) * MXU_DIM]
        r_parts.append(_dot(xs, wa_ref[b]))
        i_parts.append(_dot(xs, wx_ref[b]))
    r = jax.nn.sigmoid(jnp.concatenate(r_parts, axis=1) + ba_ref[...])
    ig = jax.nn.sigmoid(jnp.concatenate(i_parts, axis=1) + bx_ref[...])
    lam = lam_ref[...]
    softplus_neg_lam = jnp.maximum(-lam, 0.0) + jnp.log(1.0 + jnp.exp(-jnp.abs(lam)))
    log_a = (-LRU_C) * r * softplus_neg_lam
    a = jnp.exp(log_a)
    mult = jnp.sqrt(1.0 - jnp.exp(2.0 * log_a))
    b_in = mult * ig * x

    y_gate = _dot(hn, win_ref[:, d_lru:2 * d_lru])
    o_ref[...] = h + _dot(cat_ref[:, d_lru:], wout_ref[d_lru:, :])

    row = lax.broadcasted_iota(jnp.int32, (tm, d_lru), 0)
    shift = 1
    while shift < tm:
        if shift < SUBLANES:
            keep = row >= shift
            a_prev = jnp.where(keep, pltpu.roll(a, shift, axis=0), 1.0)
            b_prev = jnp.where(keep, pltpu.roll(b_in, shift, axis=0), 0.0)
        else:
            a_prev = jnp.concatenate([jnp.ones((shift, d_lru), F32), a[:tm - shift]], axis=0)
            b_prev = jnp.concatenate([jnp.zeros((shift, d_lru), F32), b_in[:tm - shift]], axis=0)
        b_in = a * b_prev + b_in
        a = a * a_prev
        shift *= 2
    hs = a * hstate[...] + b_in
    hstate[...] = hs[tm - 1:tm, :]
    cat_ref[:, 0:d_lru] = (hs * jax.nn.gelu(y_gate)).astype(BF16)
    o_ref[...] += _dot(cat_ref[:, 0:d_lru], wout_ref[0:d_lru, :])


def _mix(layer, h, g, w_in, conv_w, conv_b, wa_bd, ba, wx_bd, bx, lam, pool_w, pool_scale, w_out):
    t, d = h.shape
    d_lru = conv_w.shape[-1]
    d_pool = pool_scale.shape[-1]
    tm = min(MIX_TOKENS, t)
    row_spec = pl.BlockSpec((tm, d), lambda i: (i, 0))
    args = (g, w_in, conv_w, conv_b, wa_bd, ba, wx_bd, bx, lam, pool_w, pool_scale, w_out)
    return pl.pallas_call(
        functools.partial(_mix_kernel, d_lru=d_lru, d_pool=d_pool),
        out_shape=jax.ShapeDtypeStruct((t, d), F32),
        grid=(t // tm,),
        in_specs=[row_spec] + [_layer_spec(a, layer) for a in args],
        out_specs=row_spec,
        scratch_shapes=[
            pltpu.VMEM((tm + CONV_HALO, d_lru), F32),
            pltpu.VMEM((tm + POOL_HALO, d_pool), F32),
            pltpu.VMEM((1, d_lru), F32),
            pltpu.VMEM((tm, d_lru + d_pool), BF16),
        ],
        compiler_params=_params(("arbitrary",)),
        name="mix",
    )(h, *args)


def _sort16_network():
    pairs = []

    def merge(lo, n, r):
        step = r * 2
        if step < n:
            merge(lo, n, step)
            merge(lo + r, n, step)
            for i in range(lo + r, lo + n - r, step):
                pairs.append((i, i + r))
        else:
            pairs.append((lo, lo + r))

    def sort(lo, n):
        if n > 1:
            m = n // 2
            sort(lo, m)
            sort(lo + m, m)
            merge(lo, n, 1)

    sort(0, PEER_TOPK)
    return tuple(pairs)


_SORT16 = _sort16_network()


def _sort_desc(vals):
    vals = list(vals)
    for i, j in _SORT16:
        hi = jnp.maximum(vals[i], vals[j])
        lo = jnp.minimum(vals[i], vals[j])
        vals[i], vals[j] = hi, lo
    return vals


def _bitonic_desc(vals):
    vals = list(vals)
    n = len(vals)
    gap = n // 2
    while gap >= 1:
        for i in range(n):
            if (i // gap) % 2 == 0:
                hi = jnp.maximum(vals[i], vals[i + gap])
                lo = jnp.minimum(vals[i], vals[i + gap])
                vals[i], vals[i + gap] = hi, lo
        gap //= 2
    return vals


def _top16_rows(s_ref, idx):
    groups = [s_ref[idx, pl.ds(SUBLANES * v, SUBLANES), :] for v in range(PEER_NKEYS // SUBLANES)]
    vals = _sort_desc(groups)
    shift = SUBLANES // 2
    while shift >= 1:
        other = [pltpu.roll(v, shift, axis=0) for v in vals]
        merged = [jnp.maximum(vals[k], other[PEER_TOPK - 1 - k]) for k in range(PEER_TOPK)]
        vals = _bitonic_desc(merged)
        shift //= 2
    return vals


def _kth_largest_rows(arrays):
    n = len(arrays)
    vals = list(arrays)
    for i, j in _SORT16:
        if j < n:
            hi = jnp.maximum(vals[i], vals[j])
            lo = jnp.minimum(vals[i], vals[j])
            vals[i], vals[j] = hi, lo
    vals = vals + [None] * (PEER_TOPK - n)
    shift = SUBLANES // 2
    while shift >= 1:
        other = [None if v is None else pltpu.roll(v, shift, axis=0) for v in vals]
        merged = []
        for k in range(PEER_TOPK):
            a, b = vals[k], other[PEER_TOPK - 1 - k]
            merged.append(b if a is None else a if b is None else jnp.maximum(a, b))
        if shift > 1:
            vals = _bitonic_desc(merged)
        shift //= 2
    out = merged[0]
    for v in merged[1:]:
        out = jnp.minimum(out, v)
    return out


def _rank_desc(x, thr):
    sel = jnp.where
    c1 = thr[7] > x
    c2 = sel(c1, thr[11], thr[3]) > x
    c3 = sel(c1, sel(c2, thr[13], thr[9]), sel(c2, thr[5], thr[1])) > x
    t4 = sel(c1, sel(c2, sel(c3, thr[14], thr[12]), sel(c3, thr[10], thr[8])),
             sel(c2, sel(c3, thr[6], thr[4]), sel(c3, thr[2], thr[0])))
    c4 = t4 > x
    rank = sel(c1, 8.0, 0.0) + sel(c2, 4.0, 0.0) + sel(c3, 2.0, 0.0) + sel(c4, 1.0, 0.0)
    return sel(thr[15] > x, float(PEER_TOPK), rank)


def _pack_rows(vals):
    sub = lax.broadcasted_iota(jnp.int32, vals[0].shape, 0)
    out = vals[SUBLANES - 1]
    for k in range(SUBLANES - 2, -1, -1):
        out = jnp.where(sub == k, vals[k], out)
    return out


def _route_kernel(h_ref, g_ref, wq_ref, keys_ref, xnt_ref, r_ref, e2_ref, n_ref, d_ref, s_scr):
    tr = h_ref.shape[0]
    xn_f32 = _rmsnorm(h_ref[...], g_ref[...])
    xnt_ref[...] = xn_f32.T.astype(BF16)
    xn = xn_f32.astype(BF16)
    q = _dot(xn, wq_ref[...]).astype(BF16)
    half = keys_ref.shape[2]
    for hd in range(PEER_HEADS):
        for c in range(2):
            col = (hd * 2 + c) * half
            s_scr[hd * 2 + c] = _dot_nt(keys_ref[c], q[:, col:col + half])

    n_groups = PEER_NKEYS // SUBLANES

    def per_head(hd, carry):
        sv1 = _top16_rows(s_scr, 2 * hd)
        sv2 = _top16_rows(s_scr, 2 * hd + 1)
        p1_hi = _pack_rows(sv1[SUBLANES:])
        p2_lo = _pack_rows(sv2[:SUBLANES])
        p2_hi = _pack_rows(sv2[SUBLANES:])
        cands = [sv1[0] + p2_lo, sv1[0] + p2_hi, p1_hi + sv2[0]]
        cands += [sv1[a] + p2_lo for a in range(1, SUBLANES)]
        m1, m2 = sv1[0], sv2[0]
        tau = _kth_largest_rows(cands)
        zterms = None
        for c in cands:
            term = jnp.where(c >= tau, jnp.exp(c - (m1 + m2)), 0.0)
            zterms = term if zterms is None else zterms + term
        inv_z = 1.0 / jnp.sum(zterms, axis=0, keepdims=True)

        half_k = PEER_TOPK // 2
        count_top = jnp.full((SUBLANES, tr), float(PEER_TOPK), F32)
        for k in range(PEER_TOPK - 1, -1, -1):
            count_top = jnp.where(m1 + sv2[k] < tau, float(k), count_top)
        for v in range(n_groups):
            rows = pl.ds(SUBLANES * v, SUBLANES)
            s1g = s_scr[2 * hd, rows, :]
            s2g = s_scr[2 * hd + 1, rows, :]
            rank = _rank_desc(s2g, sv2)
            count = jnp.full((SUBLANES, tr), float(half_k), F32)
            for k in range(half_k - 1, -1, -1):
                count = jnp.where(s1g + sv2[k] < tau, float(k), count)
            count = jnp.where(s1g >= m1, count_top, count)
            n_ref[hd, rows, :] = count
            d_ref[hd, rows, :] = jnp.exp(s1g - m1) * inv_z
            s_scr[2 * hd, rows, :] = rank
            s_scr[2 * hd + 1, rows, :] = jnp.exp(s2g - m2)
        r_ref[hd] = s_scr[2 * hd].astype(BF16)
        e2_ref[hd] = s_scr[2 * hd + 1].astype(BF16)
        return carry

    lax.fori_loop(0, PEER_HEADS, per_head, 0)


def _route(layer, h, g, wq, keys):
    t, d = h.shape
    tr = min(ROUTE_TOKENS, t)
    nk = keys.shape[2]
    tab_spec = pl.BlockSpec((PEER_HEADS, nk, tr), lambda i: (0, 0, i))
    return pl.pallas_call(
        _route_kernel,
        out_shape=(
            jax.ShapeDtypeStruct((d, t), BF16),
            jax.ShapeDtypeStruct((PEER_HEADS, nk, t), BF16),
            jax.ShapeDtypeStruct((PEER_HEADS, nk, t), BF16),
            jax.ShapeDtypeStruct((PEER_HEADS, nk, t), F32),
            jax.ShapeDtypeStruct((PEER_HEADS, nk, t), F32),
        ),
        grid=(t // tr,),
        in_specs=[pl.BlockSpec((tr, d), lambda i: (i, 0)), _layer_spec(g, layer), _layer_spec(wq, layer),
                  _layer_spec(keys, layer)],
        out_specs=(pl.BlockSpec((d, tr), lambda i: (0, i)), tab_spec, tab_spec, tab_spec, tab_spec),
        scratch_shapes=[pltpu.VMEM((2 * PEER_HEADS, nk, tr), F32)],
        compiler_params=_params(("arbitrary",)),
        name="route",
    )(h, g, wq, keys)


PEER_GROUP = 2 * SUBLANES
PEER_LANE_CHUNK = MXU_DIM
PEER_PIECE = 2 * MXU_DIM


def _peer_kernel(xnt_ref, u_ref, vt_ref, r_ref, e2_ref, n_ref, d_ref, o_ref, p_scr, acc_ref, *, nkeys):
    j = pl.program_id(1)
    te, tt = p_scr.shape
    blocks_per_piece = PEER_PIECE // nkeys

    @pl.when(j == 0)
    def _():
        acc_ref[...] = jnp.zeros_like(acc_ref)

    for piece in range(te // PEER_PIECE):
        prow = pl.ds(piece * PEER_PIECE, PEER_PIECE)
        act = jax.nn.gelu(_dot(u_ref[prow, :], xnt_ref[...])).astype(BF16)
        for bi in range(blocks_per_piece):
            ii = piece * blocks_per_piece + bi
            for lc in range(tt // PEER_LANE_CHUNK):
                lanes = pl.ds(lc * PEER_LANE_CHUNK, PEER_LANE_CHUNK)
                w = [None] * (nkeys // PEER_GROUP)
                for hd in range(PEER_HEADS):
                    n_b = jnp.broadcast_to(n_ref[hd, ii:ii + 1, lanes], (PEER_GROUP, PEER_LANE_CHUNK)).astype(BF16)
                    d_b = jnp.broadcast_to(d_ref[hd, ii:ii + 1, lanes], (PEER_GROUP, PEER_LANE_CHUNK)).astype(BF16)
                    for g in range(nkeys // PEER_GROUP):
                        rows = pl.ds(g * PEER_GROUP, PEER_GROUP)
                        contrib = jnp.where(r_ref[hd, rows, lanes] < n_b, e2_ref[hd, rows, lanes] * d_b,
                                            jnp.zeros((), BF16))
                        w[g] = contrib if w[g] is None else w[g] + contrib
                for g in range(nkeys // PEER_GROUP):
                    r0 = bi * nkeys + g * PEER_GROUP
                    a_blk = act[r0:r0 + PEER_GROUP, lc * PEER_LANE_CHUNK:(lc + 1) * PEER_LANE_CHUNK]
                    p_scr[pl.ds(ii * nkeys + g * PEER_GROUP, PEER_GROUP), lanes] = w[g] * a_blk

    half_te = te // 2
    acc_ref[...] += _dot(vt_ref[:, 0:half_te], p_scr[0:half_te, :])
    acc_ref[...] += _dot(vt_ref[:, half_te:], p_scr[half_te:, :])

    @pl.when(j == pl.num_programs(1) - 1)
    def _():
        o_ref[...] = acc_ref[...].T.astype(o_ref.dtype)


def _peer(layer, xnt, u, vt, r_tab, e2_tab, n_tab, d_tab):
    d, t = xnt.shape
    n_exp = u.shape[1]
    nk = r_tab.shape[1]
    tt = min(PEER_TOKENS, t)
    te = PEER_EXPERTS_PER_STEP
    tab_spec = pl.BlockSpec((PEER_HEADS, nk, tt), lambda i, j: (0, 0, i), pipeline_mode=pl.Buffered(1))
    row_spec = pl.BlockSpec((PEER_HEADS, te // nk, tt), lambda i, j: (0, j, i))
    return pl.pallas_call(
        functools.partial(_peer_kernel, nkeys=nk),
        out_shape=jax.ShapeDtypeStruct((t, d), BF16),
        grid=(t // tt, n_exp // te),
        in_specs=[
            pl.BlockSpec((d, tt), lambda i, j: (0, i)),
            pl.BlockSpec((None, te, d), lambda i, j: (layer, j, 0)),
            pl.BlockSpec((None, d, te), lambda i, j: (layer, 0, j)),
            tab_spec, tab_spec, row_spec, row_spec,
        ],
        out_specs=pl.BlockSpec((tt, d), lambda i, j: (i, 0)),
        scratch_shapes=[
            pltpu.VMEM((te, tt), BF16),
            pltpu.VMEM((d, tt), F32),
        ],
        compiler_params=_params(("arbitrary", "arbitrary")),
        name="peer",
    )(xnt, u, vt, r_tab, e2_tab, n_tab, d_tab)


def _ple_kernel(h_ref, y_ref, p_ref, g_ref, wg_ref, wp_ref, fg_ref, o_ref, *, final_norm):
    h = h_ref[...] + y_ref[...]
    gate = jax.nn.sigmoid(_dot(_rmsnorm(h, g_ref[...]).astype(BF16), wg_ref[...]))
    out = h + gate * _dot(p_ref[...].astype(BF16), wp_ref[...])
    if final_norm:
        out = _rmsnorm(out, fg_ref[...])
    o_ref[...] = out


def _ple(layer, batch, h, y, p, g, wg, wp, final_g, final_norm):
    t, d = h.shape
    tm = min(PLE_TOKENS, t)
    final_spec = pl.BlockSpec(final_g.shape, lambda i: (0, 0), pipeline_mode=pl.Buffered(1))
    return pl.pallas_call(
        functools.partial(_ple_kernel, final_norm=final_norm),
        out_shape=jax.ShapeDtypeStruct((t, d), F32),
        grid=(t // tm,),
        in_specs=[pl.BlockSpec((tm, d), lambda i: (i, 0)), pl.BlockSpec((tm, d), lambda i: (i, 0)),
                  pl.BlockSpec((None, None, tm, p.shape[-1]), lambda i: (layer, batch, i, 0)),
                  _layer_spec(g, layer), _layer_spec(wg, layer), _layer_spec(wp, layer), final_spec],
        out_specs=pl.BlockSpec((tm, d), lambda i: (i, 0)),
        compiler_params=_params(("arbitrary",)),
        name="ple",
    )(h, y, p, g, wg, wp, final_g)


VT_EXPERTS_PER_STEP = 512


def _vt_kernel(v_ref, o_ref):
    o_ref[...] = v_ref[...].T.astype(BF16)


def _transpose_cast(v):
    layers, n_exp, d = v.shape
    te = VT_EXPERTS_PER_STEP
    return pl.pallas_call(
        _vt_kernel,
        out_shape=jax.ShapeDtypeStruct((layers, d, n_exp), BF16),
        grid=(layers, n_exp // te),
        in_specs=[pl.BlockSpec((None, te, d), lambda a, b: (a, b, 0))],
        out_specs=pl.BlockSpec((None, d, te), lambda a, b: (a, 0, b)),
        compiler_params=_params(("arbitrary", "arbitrary")),
        name="vt",
    )(v)


def _block_diag(w, group):
    layers, heads, n, _ = w.shape
    w = w.reshape(layers, heads // group, group, n, n)
    eye = jnp.eye(group, dtype=w.dtype)
    return jnp.einsum("lbgij,gk->lbgikj", w, eye).reshape(layers, heads // group, group * n, group * n)


def kernel(x, p, norm_mix_g, w_in, conv_w, conv_b, lru_wa, lru_ba, lru_wx, lru_bx, lru_lam, pool_w, pool_scale,
           w_out, norm_ffn_g, peer_wq, peer_keys, peer_u, peer_v, norm_ple_g, ple_wp, ple_wg, final_g):
    batch, seq, d = x.shape
    depth = w_in.shape[0]
    group = MXU_DIM // lru_wa.shape[-1]
    rows = lambda v: v.reshape(depth, 1, -1)
    bf = lambda v: v.astype(BF16)
    mix_args = (rows(norm_mix_g), bf(w_in), conv_w, rows(conv_b), bf(_block_diag(lru_wa, group)), rows(lru_ba),
                bf(_block_diag(lru_wx, group)), rows(lru_bx), rows(lru_lam), bf(pool_w), rows(pool_scale), bf(w_out))
    route_args = (rows(norm_ffn_g), bf(peer_wq), bf(peer_keys))
    u_all = bf(peer_u)
    vt_all = _transpose_cast(peer_v)
    ple_args = (rows(norm_ple_g), bf(ple_wg), bf(ple_wp), final_g.reshape(1, -1))
    outs = []
    for b in range(batch):
        h = x[b]
        for i in range(depth):
            h = _mix(i, h, *mix_args)
            xnt, r_tab, e2_tab, n_tab, d_tab = _route(i, h, *route_args)
            y = _peer(i, xnt, u_all, vt_all, r_tab, e2_tab, n_tab, d_tab)
            h = _ple(i, b, h, y, p, *ple_args, final_norm=(i == depth - 1))
        outs.append(h)
    return jnp.stack(outs, axis=0)
```
